```python
import jax, jax.numpy as jnp
from jax import lax
import numpy as np

D_MODEL = 2048
BATCH = 4
SEQ = 2048
DEPTH = 1
DEC_BATCH = 128
DEC_SEQ = 1
PAST_LEN = 16384
PAGE_SIZE = 128

N_MEM = 256
CONV_WIDTH = 1024
CONV_K = 3
RNN_WIDTH = 1024
RNN_HEADS = 8
RNN_HEAD_DIM = RNN_WIDTH // RNN_HEADS
RNN_CONV_K = 4
RG_C = 8.0
ATTN_HEADS = 4
ATTN_HEAD_DIM = 256
ATTN_WIDTH = ATTN_HEADS * ATTN_HEAD_DIM
N_BRANCH = 3
IN_SIZES = (CONV_WIDTH, CONV_WIDTH, CONV_WIDTH, RNN_WIDTH, RNN_WIDTH, ATTN_WIDTH, N_BRANCH * D_MODEL)
IN_SPLITS = tuple(int(s) for s in np.cumsum(IN_SIZES)[:-1])
IN_WIDTH = int(sum(IN_SIZES))
PEER_HEADS = 8
PEER_NKEYS = 128
PEER_EXPERTS = PEER_NKEYS * PEER_NKEYS
PEER_TOPK = 16
PEER_QDIM = 256
PEER_HALF = PEER_QDIM // 2
PEER_BLOCK = 128
DN_ALPHA = (2.0 * DEPTH) ** 0.25
DN_BETA = (8.0 * DEPTH) ** -0.25
LN_EPS = 1e-5

kernel_name = "hybrid_conv_rglru_mem_peer_step"


def layernorm(x, g, b):
    xf = x.astype(jnp.float32)
    mu = jnp.mean(xf, axis=-1, keepdims=True)
    var = jnp.mean(jnp.square(xf - mu), axis=-1, keepdims=True)
    return ((xf - mu) * lax.rsqrt(var + LN_EPS) * g.astype(jnp.float32) + b.astype(jnp.float32)).astype(x.dtype)


def causal_dwconv(buf, x, w):
    xp = jnp.concatenate([buf.astype(x.dtype), x], axis=1)
    k_w = w.shape[0]
    s = x.shape[1]
    y = sum(xp[:, k:k + s] * w[k] for k in range(k_w))
    return y, xp[:, xp.shape[1] - (k_w - 1):]


def rg_lru(xr, h0, wa, ba, wx, bx, lam):
    bn, s, _ = xr.shape
    xh = xr.reshape(bn, s, RNN_HEADS, RNN_HEAD_DIM)
    r = jax.nn.sigmoid(jnp.einsum('bshi,hij->bshj', xh, wa).reshape(bn, s, RNN_WIDTH) + ba)
    i = jax.nn.sigmoid(jnp.einsum('bshi,hij->bshj', xh, wx).reshape(bn, s, RNN_WIDTH) + bx)
    log_a = -RG_C * r.astype(jnp.float32) * jax.nn.softplus(-lam.astype(jnp.float32))
    a = jnp.exp(log_a)
    u = jnp.sqrt(-jnp.expm1(2.0 * log_a)) * (i * xr).astype(jnp.float32)

    def step(h, au):
        a_t, u_t = au
        h = a_t * h + u_t
        return h, h

    h_last, hs = lax.scan(step, h0.astype(jnp.float32), (jnp.swapaxes(a, 0, 1), jnp.swapaxes(u, 0, 1)))
    return jnp.swapaxes(hs, 0, 1).astype(xr.dtype), h_last.astype(h0.dtype)


def mem_attention(q, mem_k, mem_v):
    bn, s, _ = q.shape
    qh = q.reshape(bn, s, ATTN_HEADS, ATTN_HEAD_DIM)
    sc = jnp.einsum('bqhd,bkhd->bhqk', qh, mem_k.astype(q.dtype)).astype(jnp.float32) * (ATTN_HEAD_DIM ** -0.5)
    p = jax.nn.softmax(sc, axis=-1)
    o = jnp.einsum('bhqk,bkhd->bqhd', p.astype(q.dtype), mem_v.astype(q.dtype))
    return o.reshape(bn, s, ATTN_WIDTH)


def mixer_sublayer(x, mem_k, mem_v, conv_buf, rg_buf, rg_h,
                   w_in, conv_w, rg_conv_w, rg_conv_b, rg_wa, rg_ba, rg_wx, rg_bx, rg_lambda,
                   w_br_conv, w_br_rnn, w_br_attn, w_o):
    bn, s, _ = x.shape
    z = x @ w_in
    cb, cc, cx, rx, rgate, q, g = jnp.split(z, IN_SPLITS, axis=-1)
    conv_y, conv_buf_new = causal_dwconv(conv_buf, cc * cx, conv_w)
    y_conv = cb * conv_y
    xr, rg_buf_new = causal_dwconv(rg_buf, rx, rg_conv_w)
    xr = xr + rg_conv_b
    h_seq, h_new = rg_lru(xr, rg_h, rg_wa, rg_ba, rg_wx, rg_bx, rg_lambda)
    y_rnn = h_seq * jax.nn.gelu(rgate)
    y_att = mem_attention(q, mem_k, mem_v)
    gates = jax.nn.sigmoid(g.reshape(bn, s, N_BRANCH, D_MODEL))
    merged = (gates[:, :, 0] * (y_conv @ w_br_conv)
              + gates[:, :, 1] * (y_rnn @ w_br_rnn)
              + gates[:, :, 2] * (y_att @ w_br_attn))
    return merged @ w_o, conv_buf_new, rg_buf_new, h_new


def peer_block(xb, wq, keys, u_tab, v_tab):
    t = xb.shape[0]
    q = (xb @ wq).reshape(t, PEER_HEADS, 2, PEER_HALF)
    sc = jnp.einsum('thcd,hckd->thck', q, keys)
    s1, i1 = lax.top_k(sc[:, :, 0], PEER_TOPK)
    s2, i2 = lax.top_k(sc[:, :, 1], PEER_TOPK)
    comb = (s1[..., :, None] + s2[..., None, :]).reshape(t, PEER_HEADS, PEER_TOPK * PEER_TOPK)
    cand = (i1[..., :, None] * PEER_NKEYS + i2[..., None, :]).reshape(t, PEER_HEADS, PEER_TOPK * PEER_TOPK)
    top, pos = lax.top_k(comb, PEER_TOPK)
    e = jnp.take_along_axis(cand, pos, axis=-1)
    gw = jax.nn.softmax(top.astype(jnp.float32), axis=-1).astype(xb.dtype)
    u = jnp.take(u_tab, e, axis=0)
    act = jax.nn.gelu(jnp.einsum('thkd,td->thk', u, xb))
    v = jnp.take(v_tab, e, axis=0)
    return jnp.einsum('thk,thkd->td', gw * act, v)


def peer_ffn(x, wq, keys, u_tab, v_tab):
    shp = x.shape
    xf = x.reshape(-1, D_MODEL)
    n = xf.shape[0]
    pad = (-n) % PEER_BLOCK
    xb = jnp.pad(xf, ((0, pad), (0, 0))).reshape(-1, PEER_BLOCK, D_MODEL)
    yb = lax.map(lambda blk: peer_block(blk, wq, keys, u_tab, v_tab), xb)
    return yb.reshape(-1, D_MODEL)[:n].reshape(shp)


def setup_inputs(seed: int = 0) -> dict:
    key = jax.random.key(seed)
    ks = jax.random.split(key, 40)
    f32 = jnp.float32

    def nrm(k, shape, scale):
        return jax.random.normal(k, shape, f32) * scale

    a0 = jax.random.uniform(ks[19], (DEPTH, RNN_WIDTH), f32, minval=0.9, maxval=0.999)
    s0 = a0 ** (1.0 / RG_C)
    rg_lambda = jnp.log(s0) - jnp.log1p(-s0)
    return {
        "x_prompt": nrm(ks[0], (BATCH, SEQ, D_MODEL), 1.0),
        "x_sample": nrm(ks[1], (DEC_BATCH, DEC_SEQ, D_MODEL), 1.0),
        "mem_prompt": nrm(ks[2], (BATCH, N_MEM, D_MODEL), 1.0),
        "cache_mem_k": nrm(ks[3], (DEPTH, DEC_BATCH, N_MEM, ATTN_HEADS, ATTN_HEAD_DIM), 1.0),
        "cache_mem_v": nrm(ks[4], (DEPTH, DEC_BATCH, N_MEM, ATTN_HEADS, ATTN_HEAD_DIM), 1.0),
        "state_conv_z": nrm(ks[5], (DEPTH, DEC_BATCH, CONV_K - 1, CONV_WIDTH), 1.0),
        "state_rglru_conv": nrm(ks[6], (DEPTH, DEC_BATCH, RNN_CONV_K - 1, RNN_WIDTH), 1.0),
        "state_rglru_h": nrm(ks[7], (DEPTH, DEC_BATCH, RNN_WIDTH), 0.5),
        "w_in": nrm(ks[8], (DEPTH, D_MODEL, IN_WIDTH), D_MODEL ** -0.5),
        "conv_w": nrm(ks[9], (DEPTH, CONV_K, CONV_WIDTH), CONV_K ** -0.5),
        "rg_conv_w": nrm(ks[10], (DEPTH, RNN_CONV_K, RNN_WIDTH), RNN_CONV_K ** -0.5),
        "rg_conv_b": nrm(ks[11], (DEPTH, RNN_WIDTH), 0.01),
        "rg_wa": nrm(ks[12], (DEPTH, RNN_HEADS, RNN_HEAD_DIM, RNN_HEAD_DIM), RNN_HEAD_DIM ** -0.5),
        "rg_ba": nrm(ks[13], (DEPTH, RNN_WIDTH), 0.01),
        "rg_wx": nrm(ks[14], (DEPTH, RNN_HEADS, RNN_HEAD_DIM, RNN_HEAD_DIM), RNN_HEAD_DIM ** -0.5),
        "rg_bx": nrm(ks[15], (DEPTH, RNN_WIDTH), 0.01),
        "rg_lambda": rg_lambda,
        "w_mk": nrm(ks[16], (DEPTH, D_MODEL, ATTN_WIDTH), D_MODEL ** -0.5),
        "w_mv": nrm(ks[17], (DEPTH, D_MODEL, ATTN_WIDTH), D_MODEL ** -0.5 * DN_BETA),
        "w_br_conv": nrm(ks[18], (DEPTH, CONV_WIDTH, D_MODEL), CONV_WIDTH ** -0.5 * DN_BETA),
        "w_br_rnn": nrm(ks[20], (DEPTH, RNN_WIDTH, D_MODEL), RNN_WIDTH ** -0.5 * DN_BETA),
        "w_br_attn": nrm(ks[21], (DEPTH, ATTN_WIDTH, D_MODEL), ATTN_WIDTH ** -0.5 * DN_BETA),
        "w_o": nrm(ks[22], (DEPTH, D_MODEL, D_MODEL), D_MODEL ** -0.5 * DN_BETA),
        "ln1_g": 1.0 + nrm(ks[23], (DEPTH, D_MODEL), 0.01),
        "ln1_b": nrm(ks[24], (DEPTH, D_MODEL), 0.01),
        "peer_wq": nrm(ks[25], (DEPTH, D_MODEL, PEER_HEADS * PEER_QDIM), D_MODEL ** -0.5),
        "peer_keys": nrm(ks[26], (DEPTH, PEER_HEADS, 2, PEER_NKEYS, PEER_HALF), PEER_HALF ** -0.5),
        "peer_u": nrm(ks[27], (DEPTH, PEER_EXPERTS, D_MODEL), D_MODEL ** -0.5),
        "peer_v": nrm(ks[28], (DEPTH, PEER_EXPERTS, D_MODEL), PEER_HEADS ** -0.5 * DN_BETA),
        "ln2_g": 1.0 + nrm(ks[29], (DEPTH, D_MODEL), 0.01),
        "ln2_b": nrm(ks[30], (DEPTH, D_MODEL), 0.01),
    }


def reference(x_prompt, x_sample, mem_prompt, cache_mem_k, cache_mem_v, state_conv_z, state_rglru_conv,
              state_rglru_h, w_in, conv_w, rg_conv_w, rg_conv_b, rg_wa, rg_ba, rg_wx, rg_bx, rg_lambda,
              w_mk, w_mv, w_br_conv, w_br_rnn, w_br_attn, w_o, ln1_g, ln1_b,
              peer_wq, peer_keys, peer_u, peer_v, ln2_g, ln2_b):
    xp, xs = x_prompt, x_sample
    bp = xp.shape[0]
    dt = xp.dtype
    mk_l, mv_l, czp_l, rcp_l, hp_l, czs_l, rcs_l, hs_l = [], [], [], [], [], [], [], []
    for l in range(DEPTH):
        lp = (w_in[l], conv_w[l], rg_conv_w[l], rg_conv_b[l], rg_wa[l], rg_ba[l], rg_wx[l], rg_bx[l],
              rg_lambda[l], w_br_conv[l], w_br_rnn[l], w_br_attn[l], w_o[l])
        mk = (mem_prompt @ w_mk[l]).reshape(bp, N_MEM, ATTN_HEADS, ATTN_HEAD_DIM)
        mv = (mem_prompt @ w_mv[l]).reshape(bp, N_MEM, ATTN_HEADS, ATTN_HEAD_DIM)
        zc = jnp.zeros((bp, CONV_K - 1, CONV_WIDTH), dt)
        zr = jnp.zeros((bp, RNN_CONV_K - 1, RNN_WIDTH), dt)
        zh = jnp.zeros((bp, RNN_WIDTH), dt)
        o_p, cz_p, rc_p, h_p = mixer_sublayer(xp, mk, mv, zc, zr, zh, *lp)
        o_s, cz_s, rc_s, h_s = mixer_sublayer(xs, cache_mem_k[l], cache_mem_v[l], state_conv_z[l],
                                              state_rglru_conv[l], state_rglru_h[l], *lp)
        xp = layernorm(DN_ALPHA * xp + o_p, ln1_g[l], ln1_b[l])
        xs = layernorm(DN_ALPHA * xs + o_s, ln1_g[l], ln1_b[l])
        pp = (peer_wq[l], peer_keys[l], peer_u[l], peer_v[l])
        xp = layernorm(DN_ALPHA * xp + peer_ffn(xp, *pp), ln2_g[l], ln2_b[l])
        xs = layernorm(DN_ALPHA * xs + peer_ffn(xs, *pp), ln2_g[l], ln2_b[l])
        mk_l.append(mk); mv_l.append(mv)
        czp_l.append(cz_p); rcp_l.append(rc_p); hp_l.append(h_p)
        czs_l.append(cz_s); rcs_l.append(rc_s); hs_l.append(h_s)
    mem_k_prompt = jnp.stack(mk_l)
    mem_v_prompt = jnp.stack(mv_l)
    conv_z_prompt = jnp.stack(czp_l)
    rglru_conv_prompt = jnp.stack(rcp_l)
    rglru_h_prompt = jnp.stack(hp_l)
    conv_z_sample = jnp.stack(czs_l)
    rglru_conv_sample = jnp.stack(rcs_l)
    rglru_h_sample = jnp.stack(hs_l)
    return (xp, xs, mem_k_prompt, mem_v_prompt, conv_z_prompt, rglru_conv_prompt, rglru_h_prompt,
            conv_z_sample, rglru_conv_sample, rglru_h_sample)
```

```python
import functools

import jax
import jax.numpy as jnp
from jax import lax
from jax.experimental import pallas as pl
from jax.experimental.pallas import tpu as pltpu

F32 = jnp.float32
BF16 = jnp.bfloat16

D_MODEL = 2048
N_MEM = 256
CONV_WIDTH = 1024
CONV_K = 3
RNN_WIDTH = 1024
RNN_HEADS = 8
RNN_HEAD_DIM = RNN_WIDTH // RNN_HEADS
RNN_CONV_K = 4
RG_C = 8.0
ATTN_HEADS = 4
ATTN_HEAD_DIM = 256
ATTN_WIDTH = ATTN_HEADS * ATTN_HEAD_DIM
N_BRANCH = 3
PEER_HEADS = 8
PEER_NKEYS = 128
PEER_EXPERTS = PEER_NKEYS * PEER_NKEYS
PEER_TOPK = 16
PEER_HALF = 128
DEPTH = 1
DN_ALPHA = (2.0 * DEPTH) ** 0.25
LN_EPS = 1e-5

VMEM_LIMIT_V7X = 56 * 1024 * 1024
SUBLANES = 8
NOT_SELECTED_RANK = 99.0


def _params(*sem):
    return pltpu.CompilerParams(dimension_semantics=sem, vmem_limit_bytes=VMEM_LIMIT_V7X)


def _resident(shape):
    n = len(shape)
    return pl.BlockSpec(shape, lambda *_: (0,) * n, pipeline_mode=pl.Buffered(1))


def _dot(a, b):
    return jnp.dot(a, b, preferred_element_type=F32)


def _dot_nt(a, b):
    return lax.dot_general(a, b, (((1,), (1,)), ((), ())), preferred_element_type=F32)


def _dot_tn(a, b):
    return lax.dot_general(a, b, (((0,), (0,)), ((), ())), preferred_element_type=F32)


def _sigmoid(x):
    return 1.0 / (1.0 + jnp.exp(-x))


def _gelu(x):
    return 0.5 * x * (1.0 + jnp.tanh(0.7978845608028654 * (x + 0.044715 * (x * x * x))))


def _softplus(x):
    return jnp.maximum(x, 0.0) + jnp.log1p(jnp.exp(-jnp.abs(x)))


def _layernorm(y, g, b):
    mu = jnp.mean(y, axis=-1, keepdims=True)
    yc = y - mu
    var = jnp.mean(yc * yc, axis=-1, keepdims=True)
    return yc * lax.rsqrt(var + LN_EPS) * g + b


def _mem_kv_kernel(m_ref, wk_ref, wv_ref, k_ref, v_ref):
    mb = m_ref[...].astype(BF16)
    k_ref[...] = _dot(mb, wk_ref[...])
    v_ref[...] = _dot(mb, wv_ref[...])


def _mem_kv(mem, wk, wv, tm=256):
    m = mem.shape[0]
    return pl.pallas_call(
        _mem_kv_kernel,
        grid=(m // tm,),
        in_specs=[pl.BlockSpec((tm, D_MODEL), lambda i: (i, 0)),
                  _resident((D_MODEL, ATTN_WIDTH)), _resident((D_MODEL, ATTN_WIDTH))],
        out_specs=[pl.BlockSpec((tm, ATTN_WIDTH), lambda i: (i, 0))] * 2,
        out_shape=[jax.ShapeDtypeStruct((m, ATTN_WIDTH), F32)] * 2,
        compiler_params=_params("arbitrary"),
        name="mem_kv",
    )(mem, wk, wv)


def _conv_kernel(x_ref, w_ref, cw_ref, y_ref, st_ref, pb_ref, *, ts):
    @pl.when(pl.program_id(1) == 0)
    def _():
        pb_ref[0:SUBLANES, :] = jnp.zeros((SUBLANES, CONV_WIDTH), F32)

    z = _dot(x_ref[0].astype(BF16), w_ref[...])
    p = z[:, CONV_WIDTH:2 * CONV_WIDTH] * z[:, 2 * CONV_WIDTH:]
    pb_ref[SUBLANES:SUBLANES + ts, :] = p
    p1 = pb_ref[SUBLANES - 1:SUBLANES - 1 + ts, :]
    p2 = pb_ref[SUBLANES - 2:SUBLANES - 2 + ts, :]
    cw = cw_ref[...]
    y = z[:, :CONV_WIDTH] * (cw[0:1] * p2 + cw[1:2] * p1 + cw[2:3] * p)
    y_ref[0] = y.astype(BF16)
    tail = pb_ref[ts:ts + SUBLANES, :]
    st_ref[0] = tail[SUBLANES - (CONV_K - 1):]
    pb_ref[0:SUBLANES, :] = tail


def _conv_branch(x, w, cw, ts=512):
    b, s, _ = x.shape
    return pl.pallas_call(
        functools.partial(_conv_kernel, ts=ts),
        grid=(b, s // ts),
        in_specs=[pl.BlockSpec((1, ts, D_MODEL), lambda i, j: (i, j, 0)),
                  _resident((D_MODEL, 3 * CONV_WIDTH)), _resident((CONV_K, CONV_WIDTH))],
        out_specs=[pl.BlockSpec((1, ts, CONV_WIDTH), lambda i, j: (i, j, 0)),
                   pl.BlockSpec((1, CONV_K - 1, CONV_WIDTH), lambda i, j: (i, 0, 0))],
        out_shape=[jax.ShapeDtypeStruct((b, s, CONV_WIDTH), BF16),
                   jax.ShapeDtypeStruct((b, CONV_K - 1, CONV_WIDTH), F32)],
        scratch_shapes=[pltpu.VMEM((SUBLANES + ts, CONV_WIDTH), F32)],
        compiler_params=_params("arbitrary", "arbitrary"),
        name="conv_branch",
    )(x, w, cw)


def _rglru_gates(xr, wg_ref, ba, bx, lam):
    xb = xr.astype(BF16)
    gr, gi = [], []
    for h in range(RNN_HEADS):
        g = _dot(xb[:, h * RNN_HEAD_DIM:(h + 1) * RNN_HEAD_DIM], wg_ref[h])
        gr.append(g[:, :RNN_HEAD_DIM])
        gi.append(g[:, RNN_HEAD_DIM:])
    r = _sigmoid(jnp.concatenate(gr, axis=1) + ba)
    i = _sigmoid(jnp.concatenate(gi, axis=1) + bx)
    log_a = -RG_C * r * _softplus(-lam)
    a = jnp.exp(log_a)
    th = jnp.tanh(log_a)
    u = jnp.sqrt(-2.0 * th / (1.0 - th)) * (i * xr)
    return a, u


def _prefix_scan(a, u, ts):
    rows = lax.broadcasted_iota(jnp.int32, a.shape, 0)
    d = 1
    while d < ts:
        keep = rows >= d
        u = jnp.where(keep, u + a * pltpu.roll(u, d, axis=0), u)
        a = jnp.where(keep, a * pltpu.roll(a, d, axis=0), a)
        d *= 2
    return a, u


def _rnn_kernel(x_ref, w_ref, cw_ref, cb_ref, wg_ref, ba_ref, bx_ref, lam_ref,
                y_ref, buf_ref, h_ref, rb_ref, hc_ref, *, ts):
    @pl.when(pl.program_id(1) == 0)
    def _():
        rb_ref[0:SUBLANES, :] = jnp.zeros((SUBLANES, RNN_WIDTH), F32)
        hc_ref[...] = jnp.zeros((1, RNN_WIDTH), F32)

    z = _dot(x_ref[0].astype(BF16), w_ref[...])
    rx = z[:, :RNN_WIDTH]
    rb_ref[SUBLANES:SUBLANES + ts, :] = rx
    cw = cw_ref[...]
    xr = cw[3:4] * rx + cb_ref[...]
    for k in range(RNN_CONV_K - 1):
        off = SUBLANES - (RNN_CONV_K - 1) + k
        xr = xr + cw[k:k + 1] * rb_ref[off:off + ts, :]
    a, u = _rglru_gates(xr, wg_ref, ba_ref[...], bx_ref[...], lam_ref[...])
    a_pre, h0 = _prefix_scan(a, u, ts)
    h = a_pre * hc_ref[...] + h0
    y_ref[0] = (h * _gelu(z[:, RNN_WIDTH:])).astype(BF16)
    hc_ref[...] = h[ts - 1:ts, :]
    h_ref[0] = h[ts - 1:ts, :]
    tail = rb_ref[ts:ts + SUBLANES, :]
    buf_ref[0] = tail[SUBLANES - (RNN_CONV_K - 1):]
    rb_ref[0:SUBLANES, :] = tail


def _rnn_branch(x, w, cw, cb, wg, ba, bx, lam, ts=256):
    b, s, _ = x.shape
    row = _resident((1, RNN_WIDTH))
    return pl.pallas_call(
        functools.partial(_rnn_kernel, ts=ts),
        grid=(b, s // ts),
        in_specs=[pl.BlockSpec((1, ts, D_MODEL), lambda i, j: (i, j, 0)),
                  _resident((D_MODEL, 2 * RNN_WIDTH)), _resident((RNN_CONV_K, RNN_WIDTH)), row,
                  _resident((RNN_HEADS, RNN_HEAD_DIM, 2 * RNN_HEAD_DIM)), row, row, row],
        out_specs=[pl.BlockSpec((1, ts, RNN_WIDTH), lambda i, j: (i, j, 0)),
                   pl.BlockSpec((1, RNN_CONV_K - 1, RNN_WIDTH), lambda i, j: (i, 0, 0)),
                   pl.BlockSpec((1, 1, RNN_WIDTH), lambda i, j: (i, 0, 0))],
        out_shape=[jax.ShapeDtypeStruct((b, s, RNN_WIDTH), BF16),
                   jax.ShapeDtypeStruct((b, RNN_CONV_K - 1, RNN_WIDTH), F32),
                   jax.ShapeDtypeStruct((b, 1, RNN_WIDTH), F32)],
        scratch_shapes=[pltpu.VMEM((SUBLANES + ts, RNN_WIDTH), F32), pltpu.VMEM((1, RNN_WIDTH), F32)],
        compiler_params=_params("arbitrary", "arbitrary"),
        name="rnn_branch",
    )(x, w, cw, cb, wg, ba, bx, lam)


def _attn_kernel(x_ref, w_ref, k_ref, v_ref, y_ref):
    q = _dot(x_ref[0].astype(BF16), w_ref[...]).astype(BF16)
    kb = k_ref[0].astype(BF16)
    vb = v_ref[0].astype(BF16)
    outs = []
    for h in range(ATTN_HEADS):
        sl = slice(h * ATTN_HEAD_DIM, (h + 1) * ATTN_HEAD_DIM)
        sc = _dot_nt(q[:, sl], kb[:, sl]) * (ATTN_HEAD_DIM ** -0.5)
        e = jnp.exp(sc - jnp.max(sc, axis=-1, keepdims=True))
        p = e / jnp.sum(e, axis=-1, keepdims=True)
        outs.append(_dot(p.astype(BF16), vb[:, sl]))
    y_ref[0] = jnp.concatenate(outs, axis=1).astype(BF16)


def _attn_branch(x, w, mk, mv, ts=512):
    b, s, _ = x.shape
    return pl.pallas_call(
        _attn_kernel,
        grid=(b, s // ts),
        in_specs=[pl.BlockSpec((1, ts, D_MODEL), lambda i, j: (i, j, 0)),
                  _resident((D_MODEL, ATTN_WIDTH)),
                  pl.BlockSpec((1, N_MEM, ATTN_WIDTH), lambda i, j: (i, 0, 0)),
                  pl.BlockSpec((1, N_MEM, ATTN_WIDTH), lambda i, j: (i, 0, 0))],
        out_specs=pl.BlockSpec((1, ts, ATTN_WIDTH), lambda i, j: (i, j, 0)),
        out_shape=jax.ShapeDtypeStruct((b, s, ATTN_WIDTH), BF16),
        compiler_params=_params("arbitrary", "arbitrary"),
        name="attn_branch",
    )(x, w, mk, mv)


def _sample_mix_kernel(z_ref, cst_ref, rst_ref, h0_ref, k_ref, v_ref, ccw_ref, rcw_ref, rcb_ref, wg_ref,
                       ba_ref, bx_ref, lam_ref, yc_ref, yr_ref, ya_ref, cz_ref, rc_ref, h_ref):
    z = z_ref[...]
    p = z[:, CONV_WIDTH:2 * CONV_WIDTH] * z[:, 2 * CONV_WIDTH:3 * CONV_WIDTH]
    cst = cst_ref[...]
    ccw = ccw_ref[...]
    conv_y = ccw[0:1] * cst[:, 0, :] + ccw[1:2] * cst[:, 1, :] + ccw[2:3] * p
    yc_ref[...] = (z[:, :CONV_WIDTH] * conv_y).astype(BF16)
    cz_ref[:, 0, :] = cst[:, 1, :]
    cz_ref[:, 1, :] = p
    rx = z[:, 3 * CONV_WIDTH:3 * CONV_WIDTH + RNN_WIDTH]
    rst = rst_ref[...]
    rcw = rcw_ref[...]
    xr = rcw[0:1] * rst[:, 0, :] + rcw[1:2] * rst[:, 1, :] + rcw[2:3] * rst[:, 2, :] + rcw[3:4] * rx + rcb_ref[...]
    a, u = _rglru_gates(xr, wg_ref, ba_ref[...], bx_ref[...], lam_ref[...])
    h = a * h0_ref[...] + u
    rgate = z[:, 3 * CONV_WIDTH + RNN_WIDTH:3 * CONV_WIDTH + 2 * RNN_WIDTH]
    yr_ref[...] = (h * _gelu(rgate)).astype(BF16)
    h_ref[...] = h
    rc_ref[:, 0, :] = rst[:, 1, :]
    rc_ref[:, 1, :] = rst[:, 2, :]
    rc_ref[:, 2, :] = rx
    q = z[:, 3 * CONV_WIDTH + 2 * RNN_WIDTH:]
    prod = k_ref[...] * q[:, None, :]
    outs = []
    for hd in range(ATTN_HEADS):
        sl = slice(hd * ATTN_HEAD_DIM, (hd + 1) * ATTN_HEAD_DIM)
        sc = jnp.sum(prod[:, :, sl], axis=-1, keepdims=True) * (ATTN_HEAD_DIM ** -0.5)
        e = jnp.exp(sc - jnp.max(sc, axis=1, keepdims=True))
        pr = e / jnp.sum(e, axis=1, keepdims=True)
        outs.append(jnp.sum(pr * v_ref[:, :, sl], axis=1))
    ya_ref[...] = jnp.concatenate(outs, axis=1).astype(BF16)


def _sample_mix(z, cst, rst, h0, ck, cv, ccw, rcw, rcb, wg, ba, bx, lam, tb=8):
    n = z.shape[0]
    zw = z.shape[1]
    row = _resident((1, RNN_WIDTH))

    def tok(*tail):
        nd = len(tail)
        return pl.BlockSpec((tb,) + tail, lambda i: (i,) + (0,) * nd)

    return pl.pallas_call(
        _sample_mix_kernel,
        grid=(n // tb,),
        in_specs=[tok(zw), tok(CONV_K - 1, CONV_WIDTH), tok(RNN_CONV_K - 1, RNN_WIDTH), tok(RNN_WIDTH),
                  tok(N_MEM, ATTN_WIDTH), tok(N_MEM, ATTN_WIDTH),
                  _resident((CONV_K, CONV_WIDTH)), _resident((RNN_CONV_K, RNN_WIDTH)), row,
                  _resident((RNN_HEADS, RNN_HEAD_DIM, 2 * RNN_HEAD_DIM)), row, row, row],
        out_specs=[tok(CONV_WIDTH), tok(RNN_WIDTH), tok(ATTN_WIDTH),
                   tok(CONV_K - 1, CONV_WIDTH), tok(RNN_CONV_K - 1, RNN_WIDTH), tok(RNN_WIDTH)],
        out_shape=[jax.ShapeDtypeStruct((n, CONV_WIDTH), BF16), jax.ShapeDtypeStruct((n, RNN_WIDTH), BF16),
                   jax.ShapeDtypeStruct((n, ATTN_WIDTH), BF16),
                   jax.ShapeDtypeStruct((n, CONV_K - 1, CONV_WIDTH), F32),
                   jax.ShapeDtypeStruct((n, RNN_CONV_K - 1, RNN_WIDTH), F32),
                   jax.ShapeDtypeStruct((n, RNN_WIDTH), F32)],
        compiler_params=_params("arbitrary"),
        name="sample_mix",
    )(z, cst, rst, h0, ck, cv, ccw, rcw, rcb, wg, ba, bx, lam)


def _proj_kernel(x_ref, w_ref, o_ref):
    o_ref[...] = _dot(x_ref[...].astype(BF16), w_ref[...])


def _proj(x, w, tn=1024):
    m, k = x.shape
    n = w.shape[1]
    return pl.pallas_call(
        _proj_kernel,
        grid=(n // tn,),
        in_specs=[_resident((m, k)), pl.BlockSpec((k, tn), lambda j: (0, j))],
        out_specs=pl.BlockSpec((m, tn), lambda j: (0, j)),
        out_shape=jax.ShapeDtypeStruct((m, n), F32),
        compiler_params=_params("arbitrary"),
        name="sample_proj",
    )(x, w)


def _merge_kernel(x_ref, yc_ref, yr_ref, ya_ref, wg0_ref, wg1_ref, wg2_ref, wc_ref, wr_ref, wa_ref, wo_ref,
                  g_ref, b_ref, o_ref, xb_ref, acc_ref):
    n = pl.program_id(1)

    @pl.when(n == 0)
    def _():
        xb_ref[...] = x_ref[...].astype(BF16)
        acc_ref[...] = jnp.zeros_like(acc_ref)

    xb = xb_ref[...]
    merged = (_sigmoid(_dot(xb, wg0_ref[...])) * _dot(yc_ref[...], wc_ref[...])
              + _sigmoid(_dot(xb, wg1_ref[...])) * _dot(yr_ref[...], wr_ref[...])
              + _sigmoid(_dot(xb, wg2_ref[...])) * _dot(ya_ref[...], wa_ref[...]))
    acc_ref[...] += _dot(merged.astype(BF16), wo_ref[...])

    @pl.when(n == pl.num_programs(1) - 1)
    def _():
        o_ref[...] = _layernorm(DN_ALPHA * x_ref[...] + acc_ref[...], g_ref[...], b_ref[...])


def _merge(x, yc, yr, ya, wgate, wc, wr, wa, wo, g, b, tm, tn=256):
    t = x.shape[0]
    nb = D_MODEL // tn
    tok_f = pl.BlockSpec((tm, D_MODEL), lambda i, n: (i, 0))
    tok_h = pl.BlockSpec((tm, CONV_WIDTH), lambda i, n: (i, 0))
    gate = [pl.BlockSpec((D_MODEL, tn), functools.partial(lambda i, n, k: (0, k * nb + n), k=k))
            for k in range(N_BRANCH)]
    br = pl.BlockSpec((CONV_WIDTH, tn), lambda i, n: (0, n))
    row = _resident((1, D_MODEL))
    return pl.pallas_call(
        _merge_kernel,
        grid=(t // tm, nb),
        in_specs=[tok_f, tok_h, tok_h, tok_h] + gate + [br, br, br,
                  pl.BlockSpec((tn, D_MODEL), lambda i, n: (n, 0)), row, row],
        out_specs=tok_f,
        out_shape=jax.ShapeDtypeStruct((t, D_MODEL), F32),
        scratch_shapes=[pltpu.VMEM((tm, D_MODEL), BF16), pltpu.VMEM((tm, D_MODEL), F32)],
        compiler_params=_params("arbitrary", "arbitrary"),
        name="merge",
    )(x, yc, yr, ya, wgate, wgate, wgate, wc, wr, wa, wo, g, b)


def _top_rows(sc, k):
    n = sc.shape[0]
    rows = lax.broadcasted_iota(jnp.int32, sc.shape, 0)
    rank = jnp.full(sc.shape, NOT_SELECTED_RANK, F32)
    rem = sc
    vals = []
    for r in range(k):
        m = jnp.max(rem, axis=0, keepdims=True)
        first = jnp.min(jnp.where(rem == m, rows, n), axis=0, keepdims=True)
        hit = rows == first
        rank = jnp.where(hit, float(r), rank)
        rem = jnp.where(hit, -jnp.inf, rem)
        vals.append(m)
    return jnp.concatenate(vals, axis=0), rank


_WIDE_R1 = 8


def _peer_sel_kernel(x_ref, wq_ref, keys_ref, r2_ref, b2_ref, n1_ref, a1_ref):
    q = _dot(x_ref[...].astype(BF16), wq_ref[...]).astype(BF16)
    ts = q.shape[0]
    for h in range(PEER_HEADS):
        s1 = _dot_nt(keys_ref[2 * h], q[:, (2 * h) * PEER_HALF:(2 * h + 1) * PEER_HALF])
        s2 = _dot_nt(keys_ref[2 * h + 1], q[:, (2 * h + 1) * PEER_HALF:(2 * h + 2) * PEER_HALF])
        v1, rank1 = _top_rows(s1, PEER_TOPK)
        v2, rank2 = _top_rows(s2, PEER_TOPK)
        blocks = [v1[0:1] + v2]
        blocks += [v1[r:r + 1] + v2[:SUBLANES] for r in range(1, _WIDE_R1)]
        blocks.append(v1[_WIDE_R1:] + v2[0:1])
        cand = jnp.concatenate(blocks, axis=0)
        _, crank = _top_rows(cand, PEER_TOPK)
        chosen = crank < NOT_SELECTED_RANK
        z = jnp.sum(jnp.where(chosen, jnp.exp(cand - cand[0:1]), 0.0), axis=0, keepdims=True)
        cnt = jnp.where(chosen, 1.0, 0.0)
        counts = [jnp.sum(cnt[0:PEER_TOPK], axis=0, keepdims=True)]
        for r in range(1, _WIDE_R1):
            lo = PEER_TOPK + (r - 1) * SUBLANES
            counts.append(jnp.sum(cnt[lo:lo + SUBLANES], axis=0, keepdims=True))
        lo = PEER_TOPK + (_WIDE_R1 - 1) * SUBLANES
        counts += [cnt[lo + r:lo + r + 1] for r in range(PEER_TOPK - _WIDE_R1)]
        n1 = jnp.zeros((PEER_NKEYS, ts), F32)
        for r in range(PEER_TOPK):
            n1 = jnp.where(rank1 == float(r), counts[r], n1)
        r2_ref[h] = rank2
        n1_ref[h] = n1
        a1_ref[h] = jnp.exp(s1 - v1[0:1])
        b2_ref[h] = jnp.exp(s2 - v2[0:1]) / z


def _peer_sel(x1, wq, keys, ts):
    t = x1.shape[0]
    sel = pl.BlockSpec((PEER_HEADS, PEER_NKEYS, ts), lambda i: (0, 0, i))
    shp = jax.ShapeDtypeStruct((PEER_HEADS, PEER_NKEYS, t), F32)
    return pl.pallas_call(
        _peer_sel_kernel,
        grid=(t // ts,),
        in_specs=[pl.BlockSpec((ts, D_MODEL), lambda i: (i, 0)),
                  _resident((D_MODEL, PEER_HEADS * 2 * PEER_HALF)),
                  _resident((PEER_HEADS * 2, PEER_NKEYS, PEER_HALF))],
        out_specs=[sel] * 4,
        out_shape=[shp] * 4,
        compiler_params=_params("arbitrary"),
        name="peer_sel",
    )(x1, wq, keys)


def _peer_ffn_kernel(x_ref, r2_ref, b2_ref, n1_ref, a1_ref, u_ref, v_ref, g_ref, b_ref, o_ref, xt_ref, acc_ref,
                     *, rows_per_chunk):
    c = pl.program_id(1)

    @pl.when(c == 0)
    def _():
        xt_ref[...] = x_ref[...].T.astype(BF16)
        acc_ref[...] = jnp.zeros_like(acc_ref)

    act = _gelu(_dot(u_ref[...], xt_ref[...]))
    parts = []
    for ii in range(rows_per_chunk):
        i = c * rows_per_chunk + ii
        w = None
        for h in range(PEER_HEADS):
            nrow = n1_ref[h, pl.ds(i, 1), :]
            arow = a1_ref[h, pl.ds(i, 1), :]
            wh = jnp.where(r2_ref[h] < nrow, b2_ref[h], 0.0) * arow
            w = wh if w is None else w + wh
        parts.append((w * act[ii * PEER_NKEYS:(ii + 1) * PEER_NKEYS]).astype(BF16))
    pt = jnp.concatenate(parts, axis=0)
    acc_ref[...] += _dot_tn(pt, v_ref[...])

    @pl.when(c == pl.num_programs(1) - 1)
    def _():
        o_ref[...] = _layernorm(DN_ALPHA * x_ref[...] + acc_ref[...], g_ref[...], b_ref[...])


def _peer_ffn(x1, r2, b2, n1, a1, u, v, g, b, tm, ec=512):
    t = x1.shape[0]
    tok = pl.BlockSpec((tm, D_MODEL), lambda i, c: (i, 0))
    sel = pl.BlockSpec((PEER_HEADS, PEER_NKEYS, tm), lambda i, c: (0, 0, i))
    tab = pl.BlockSpec((ec, D_MODEL), lambda i, c: (c, 0))
    row = _resident((1, D_MODEL))
    return pl.pallas_call(
        functools.partial(_peer_ffn_kernel, rows_per_chunk=ec // PEER_NKEYS),
        grid=(t // tm, PEER_EXPERTS // ec),
        in_specs=[tok, sel, sel, sel, sel, tab, tab, row, row],
        out_specs=tok,
        out_shape=jax.ShapeDtypeStruct((t, D_MODEL), F32),
        scratch_shapes=[pltpu.VMEM((D_MODEL, tm), BF16), pltpu.VMEM((tm, D_MODEL), F32)],
        compiler_params=_params("arbitrary", "arbitrary"),
        name="peer_ffn",
    )(x1, r2, b2, n1, a1, u, v, g, b)


def _channel_sublayers(x, yc, yr, ya, wts, tm):
    x1 = _merge(x, yc, yr, ya, wts["gate"], wts["br_conv"], wts["br_rnn"], wts["br_attn"], wts["o"],
                wts["ln1_g"], wts["ln1_b"], tm)
    r2, b2, n1, a1 = _peer_sel(x1, wts["peer_wq"], wts["peer_keys"], min(tm, 256))
    return _peer_ffn(x1, r2, b2, n1, a1, wts["peer_u"], wts["peer_v"], wts["ln2_g"], wts["ln2_b"], tm)


def kernel(x_prompt, x_sample, mem_prompt, cache_mem_k, cache_mem_v, state_conv_z, state_rglru_conv, state_rglru_h, w_in, conv_w, rg_conv_w, rg_conv_b, rg_wa, rg_ba, rg_wx, rg_bx, rg_lambda, w_mk, w_mv, w_br_conv, w_br_rnn, w_br_attn, w_o, ln1_g, ln1_b, peer_wq, peer_keys, peer_u, peer_v, ln2_g, ln2_b):
    assert w_in.shape[0] == DEPTH == 1
    bp, sp, _ = x_prompt.shape
    bs = x_sample.shape[0]
    l = 0
    c3 = 3 * CONV_WIDTH
    r2w = 2 * RNN_WIDTH
    win = w_in[l].astype(BF16)
    w_conv, w_rnn = win[:, :c3], win[:, c3:c3 + r2w]
    w_q = win[:, c3 + r2w:c3 + r2w + ATTN_WIDTH]
    row = lambda a: a[l].reshape(1, -1)
    wg = jnp.concatenate([rg_wa[l], rg_wx[l]], axis=-1).astype(BF16)
    wts = {
        "gate": win[:, c3 + r2w + ATTN_WIDTH:],
        "br_conv": w_br_conv[l].astype(BF16), "br_rnn": w_br_rnn[l].astype(BF16),
        "br_attn": w_br_attn[l].astype(BF16), "o": w_o[l].astype(BF16),
        "ln1_g": row(ln1_g), "ln1_b": row(ln1_b), "ln2_g": row(ln2_g), "ln2_b": row(ln2_b),
        "peer_wq": peer_wq[l].astype(BF16),
        "peer_keys": peer_keys[l].reshape(PEER_HEADS * 2, PEER_NKEYS, PEER_HALF).astype(BF16),
        "peer_u": peer_u[l].astype(BF16), "peer_v": peer_v[l].astype(BF16),
    }
    rnn_args = (rg_conv_w[l], row(rg_conv_b), wg, row(rg_ba), row(rg_bx), row(rg_lambda))

    mk, mv = _mem_kv(mem_prompt.reshape(bp * N_MEM, D_MODEL), w_mk[l].astype(BF16), w_mv[l].astype(BF16))
    yc_p, cz_p = _conv_branch(x_prompt, w_conv, conv_w[l])
    yr_p, rc_p, h_p = _rnn_branch(x_prompt, w_rnn, *rnn_args)
    ya_p = _attn_branch(x_prompt, w_q, mk.reshape(bp, N_MEM, ATTN_WIDTH), mv.reshape(bp, N_MEM, ATTN_WIDTH))
    tp = bp * sp
    y_p = _channel_sublayers(x_prompt.reshape(tp, D_MODEL), yc_p.reshape(tp, CONV_WIDTH),
                             yr_p.reshape(tp, RNN_WIDTH), ya_p.reshape(tp, ATTN_WIDTH), wts, 512)

    xs = x_sample.reshape(bs, D_MODEL)
    z_s = _proj(xs, win[:, :c3 + r2w + ATTN_WIDTH])
    yc_s, yr_s, ya_s, cz_s, rc_s, h_s = _sample_mix(
        z_s, state_conv_z[l], state_rglru_conv[l], state_rglru_h[l],
        cache_mem_k[l].reshape(bs, N_MEM, ATTN_WIDTH), cache_mem_v[l].reshape(bs, N_MEM, ATTN_WIDTH),
        conv_w[l], *rnn_args)
    y_s = _channel_sublayers(xs, yc_s, yr_s, ya_s, wts, bs)

    hd = (ATTN_HEADS, ATTN_HEAD_DIM)
    return (y_p.reshape(bp, sp, D_MODEL), y_s.reshape(bs, 1, D_MODEL),
            mk.reshape(1, bp, N_MEM, *hd), mv.reshape(1, bp, N_MEM, *hd),
            cz_p[None], rc_p[None], h_p.reshape(1, bp, RNN_WIDTH),
            cz_s[None], rc_s[None], h_s[None])
```

```python
import functools

import jax
import jax.numpy as jnp
from jax import lax
from jax.experimental import pallas as pl
from jax.experimental.pallas import tpu as pltpu

F32 = jnp.float32
BF16 = jnp.bfloat16

D_MODEL = 2048
N_MEM = 256
CONV_WIDTH = 1024
CONV_K = 3
RNN_WIDTH = 1024
RNN_HEADS = 8
RNN_HEAD_DIM = RNN_WIDTH // RNN_HEADS
RNN_CONV_K = 4
RG_C = 8.0
ATTN_HEADS = 4
ATTN_HEAD_DIM = 256
ATTN_WIDTH = ATTN_HEADS * ATTN_HEAD_DIM
N_BRANCH = 3
PEER_HEADS = 8
PEER_NKEYS = 128
PEER_EXPERTS = PEER_NKEYS * PEER_NKEYS
PEER_TOPK = 16
PEER_HALF = 128
DEPTH = 1
DN_ALPHA = (2.0 * DEPTH) ** 0.25
LN_EPS = 1e-5

VMEM_LIMIT_V7X = 56 * 1024 * 1024
SUBLANES = 8
BF16_ROWS = 2 * SUBLANES
NOT_SELECTED_RANK = 99.0


def _params(*sem):
    return pltpu.CompilerParams(dimension_semantics=sem, vmem_limit_bytes=VMEM_LIMIT_V7X)


def _resident(shape):
    n = len(shape)
    return pl.BlockSpec(shape, lambda *_: (0,) * n, pipeline_mode=pl.Buffered(1))


def _dot(a, b):
    return jnp.dot(a, b, preferred_element_type=F32)


def _dot_nt(a, b):
    return lax.dot_general(a, b, (((1,), (1,)), ((), ())), preferred_element_type=F32)


def _dot_tn(a, b):
    return lax.dot_general(a, b, (((0,), (0,)), ((), ())), preferred_element_type=F32)


def _sigmoid(x):
    return 1.0 / (1.0 + jnp.exp(-x))


def _gelu(x):
    return 0.5 * x * (1.0 + jnp.tanh(0.7978845608028654 * (x + 0.044715 * (x * x * x))))


def _softplus(x):
    return jnp.maximum(x, 0.0) + jnp.log1p(jnp.exp(-jnp.abs(x)))


def _layernorm(y, g, b):
    mu = jnp.mean(y, axis=-1, keepdims=True)
    yc = y - mu
    var = jnp.mean(yc * yc, axis=-1, keepdims=True)
    return yc * lax.rsqrt(var + LN_EPS) * g + b


def _mem_kv_kernel(m_ref, wk_ref, wv_ref, k_ref, v_ref):
    mb = m_ref[...].astype(BF16)
    k_ref[...] = _dot(mb, wk_ref[...])
    v_ref[...] = _dot(mb, wv_ref[...])


def _mem_kv(mem, wk, wv, tm=256):
    m = mem.shape[0]
    return pl.pallas_call(
        _mem_kv_kernel,
        grid=(m // tm,),
        in_specs=[pl.BlockSpec((tm, D_MODEL), lambda i: (i, 0)),
                  _resident((D_MODEL, ATTN_WIDTH)), _resident((D_MODEL, ATTN_WIDTH))],
        out_specs=[pl.BlockSpec((tm, ATTN_WIDTH), lambda i: (i, 0))] * 2,
        out_shape=[jax.ShapeDtypeStruct((m, ATTN_WIDTH), F32)] * 2,
        compiler_params=_params("arbitrary"),
        name="mem_kv",
    )(mem, wk, wv)


def _conv_kernel(x_ref, w_ref, cw_ref, y_ref, st_ref, pb_ref, *, ts):
    @pl.when(pl.program_id(1) == 0)
    def _():
        pb_ref[0:SUBLANES, :] = jnp.zeros((SUBLANES, CONV_WIDTH), F32)

    z = _dot(x_ref[0].astype(BF16), w_ref[...])
    p = z[:, CONV_WIDTH:2 * CONV_WIDTH] * z[:, 2 * CONV_WIDTH:]
    pb_ref[SUBLANES:SUBLANES + ts, :] = p
    p1 = pb_ref[SUBLANES - 1:SUBLANES - 1 + ts, :]
    p2 = pb_ref[SUBLANES - 2:SUBLANES - 2 + ts, :]
    cw = cw_ref[...]
    y = z[:, :CONV_WIDTH] * (cw[0:1] * p2 + cw[1:2] * p1 + cw[2:3] * p)
    y_ref[0] = y.astype(BF16)
    tail = pb_ref[ts:ts + SUBLANES, :]
    st_ref[0] = tail[SUBLANES - (CONV_K - 1):]
    pb_ref[0:SUBLANES, :] = tail


def _conv_branch(x, w, cw, ts=512):
    b, s, _ = x.shape
    return pl.pallas_call(
        functools.partial(_conv_kernel, ts=ts),
        grid=(b, s // ts),
        in_specs=[pl.BlockSpec((1, ts, D_MODEL), lambda i, j: (i, j, 0)),
                  _resident((D_MODEL, 3 * CONV_WIDTH)), _resident((CONV_K, CONV_WIDTH))],
        out_specs=[pl.BlockSpec((1, ts, CONV_WIDTH), lambda i, j: (i, j, 0)),
                   pl.BlockSpec((1, CONV_K - 1, CONV_WIDTH), lambda i, j: (i, 0, 0))],
        out_shape=[jax.ShapeDtypeStruct((b, s, CONV_WIDTH), BF16),
                   jax.ShapeDtypeStruct((b, CONV_K - 1, CONV_WIDTH), F32)],
        scratch_shapes=[pltpu.VMEM((SUBLANES + ts, CONV_WIDTH), F32)],
        compiler_params=_params("arbitrary", "arbitrary"),
        name="conv_branch",
    )(x, w, cw)


def _rglru_gates(xr, wg_ref, ba, bx, lam):
    xb = xr.astype(BF16)
    gr, gi = [], []
    for h in range(RNN_HEADS):
        g = _dot(xb[:, h * RNN_HEAD_DIM:(h + 1) * RNN_HEAD_DIM], wg_ref[h])
        gr.append(g[:, :RNN_HEAD_DIM])
        gi.append(g[:, RNN_HEAD_DIM:])
    r = _sigmoid(jnp.concatenate(gr, axis=1) + ba)
    i = _sigmoid(jnp.concatenate(gi, axis=1) + bx)
    log_a = -RG_C * r * _softplus(-lam)
    a = jnp.exp(log_a)
    th = jnp.tanh(log_a)
    u = jnp.sqrt(-2.0 * th / (1.0 - th)) * (i * xr)
    return a, u


def _prefix_scan(a, u, ts):
    rows = lax.broadcasted_iota(jnp.int32, a.shape, 0)
    d = 1
    while d < ts:
        keep = rows >= d
        u = jnp.where(keep, u + a * pltpu.roll(u, d, axis=0), u)
        a = jnp.where(keep, a * pltpu.roll(a, d, axis=0), a)
        d *= 2
    return a, u


def _rnn_kernel(x_ref, w_ref, cw_ref, cb_ref, wg_ref, ba_ref, bx_ref, lam_ref,
                y_ref, buf_ref, h_ref, rb_ref, hc_ref, *, ts):
    @pl.when(pl.program_id(1) == 0)
    def _():
        rb_ref[0:SUBLANES, :] = jnp.zeros((SUBLANES, RNN_WIDTH), F32)
        hc_ref[...] = jnp.zeros((1, RNN_WIDTH), F32)

    z = _dot(x_ref[0].astype(BF16), w_ref[...])
    rx = z[:, :RNN_WIDTH]
    rb_ref[SUBLANES:SUBLANES + ts, :] = rx
    cw = cw_ref[...]
    xr = cw[3:4] * rx + cb_ref[...]
    for k in range(RNN_CONV_K - 1):
        off = SUBLANES - (RNN_CONV_K - 1) + k
        xr = xr + cw[k:k + 1] * rb_ref[off:off + ts, :]
    a, u = _rglru_gates(xr, wg_ref, ba_ref[...], bx_ref[...], lam_ref[...])
    a_pre, h0 = _prefix_scan(a, u, ts)
    h = a_pre * hc_ref[...] + h0
    y_ref[0] = (h * _gelu(z[:, RNN_WIDTH:])).astype(BF16)
    hc_ref[...] = h[ts - 1:ts, :]
    h_ref[0] = h[ts - 1:ts, :]
    tail = rb_ref[ts:ts + SUBLANES, :]
    buf_ref[0] = tail[SUBLANES - (RNN_CONV_K - 1):]
    rb_ref[0:SUBLANES, :] = tail


def _rnn_branch(x, w, cw, cb, wg, ba, bx, lam, ts=256):
    b, s, _ = x.shape
    row = _resident((1, RNN_WIDTH))
    return pl.pallas_call(
        functools.partial(_rnn_kernel, ts=ts),
        grid=(b, s // ts),
        in_specs=[pl.BlockSpec((1, ts, D_MODEL), lambda i, j: (i, j, 0)),
                  _resident((D_MODEL, 2 * RNN_WIDTH)), _resident((RNN_CONV_K, RNN_WIDTH)), row,
                  _resident((RNN_HEADS, RNN_HEAD_DIM, 2 * RNN_HEAD_DIM)), row, row, row],
        out_specs=[pl.BlockSpec((1, ts, RNN_WIDTH), lambda i, j: (i, j, 0)),
                   pl.BlockSpec((1, RNN_CONV_K - 1, RNN_WIDTH), lambda i, j: (i, 0, 0)),
                   pl.BlockSpec((1, 1, RNN_WIDTH), lambda i, j: (i, 0, 0))],
        out_shape=[jax.ShapeDtypeStruct((b, s, RNN_WIDTH), BF16),
                   jax.ShapeDtypeStruct((b, RNN_CONV_K - 1, RNN_WIDTH), F32),
                   jax.ShapeDtypeStruct((b, 1, RNN_WIDTH), F32)],
        scratch_shapes=[pltpu.VMEM((SUBLANES + ts, RNN_WIDTH), F32), pltpu.VMEM((1, RNN_WIDTH), F32)],
        compiler_params=_params("arbitrary", "arbitrary"),
        name="rnn_branch",
    )(x, w, cw, cb, wg, ba, bx, lam)


def _attn_kernel(x_ref, w_ref, k_ref, v_ref, y_ref):
    q = _dot(x_ref[0].astype(BF16), w_ref[...]).astype(BF16)
    kb = k_ref[0].astype(BF16)
    vb = v_ref[0].astype(BF16)
    outs = []
    for h in range(ATTN_HEADS):
        sl = slice(h * ATTN_HEAD_DIM, (h + 1) * ATTN_HEAD_DIM)
        sc = _dot_nt(q[:, sl], kb[:, sl]) * (ATTN_HEAD_DIM ** -0.5)
        e = jnp.exp(sc - jnp.max(sc, axis=-1, keepdims=True))
        p = e / jnp.sum(e, axis=-1, keepdims=True)
        outs.append(_dot(p.astype(BF16), vb[:, sl]))
    y_ref[0] = jnp.concatenate(outs, axis=1).astype(BF16)


def _attn_branch(x, w, mk, mv, ts=512):
    b, s, _ = x.shape
    return pl.pallas_call(
        _attn_kernel,
        grid=(b, s // ts),
        in_specs=[pl.BlockSpec((1, ts, D_MODEL), lambda i, j: (i, j, 0)),
                  _resident((D_MODEL, ATTN_WIDTH)),
                  pl.BlockSpec((1, N_MEM, ATTN_WIDTH), lambda i, j: (i, 0, 0)),
                  pl.BlockSpec((1, N_MEM, ATTN_WIDTH), lambda i, j: (i, 0, 0))],
        out_specs=pl.BlockSpec((1, ts, ATTN_WIDTH), lambda i, j: (i, j, 0)),
        out_shape=jax.ShapeDtypeStruct((b, s, ATTN_WIDTH), BF16),
        compiler_params=_params("arbitrary", "arbitrary"),
        name="attn_branch",
    )(x, w, mk, mv)


def _sample_seq_kernel(z_ref, cst_ref, rst_ref, h0_ref, ccw_ref, rcw_ref, rcb_ref, wg_ref,
                       ba_ref, bx_ref, lam_ref, yc_ref, yr_ref, cz_ref, rc_ref, h_ref):
    z = z_ref[...]
    p = z[:, CONV_WIDTH:2 * CONV_WIDTH] * z[:, 2 * CONV_WIDTH:3 * CONV_WIDTH]
    cst = cst_ref[...]
    ccw = ccw_ref[...]
    conv_y = ccw[0:1] * cst[:, 0, :] + ccw[1:2] * cst[:, 1, :] + ccw[2:3] * p
    yc_ref[...] = (z[:, :CONV_WIDTH] * conv_y).astype(BF16)
    cz_ref[:, 0, :] = cst[:, 1, :]
    cz_ref[:, 1, :] = p
    rx = z[:, 3 * CONV_WIDTH:3 * CONV_WIDTH + RNN_WIDTH]
    rst = rst_ref[...]
    rcw = rcw_ref[...]
    xr = rcw[0:1] * rst[:, 0, :] + rcw[1:2] * rst[:, 1, :] + rcw[2:3] * rst[:, 2, :] + rcw[3:4] * rx + rcb_ref[...]
    a, u = _rglru_gates(xr, wg_ref, ba_ref[...], bx_ref[...], lam_ref[...])
    h = a * h0_ref[...] + u
    yr_ref[...] = (h * _gelu(z[:, 3 * CONV_WIDTH + RNN_WIDTH:])).astype(BF16)
    h_ref[...] = h
    rc_ref[:, 0, :] = rst[:, 1, :]
    rc_ref[:, 1, :] = rst[:, 2, :]
    rc_ref[:, 2, :] = rx


def _sample_seq(z, cst, rst, h0, ccw, rcw, rcb, wg, ba, bx, lam, tb=32):
    n = z.shape[0]
    row = _resident((1, RNN_WIDTH))

    def tok(*tail):
        nd = len(tail)
        return pl.BlockSpec((tb,) + tail, lambda i: (i,) + (0,) * nd)

    return pl.pallas_call(
        _sample_seq_kernel,
        grid=(n // tb,),
        in_specs=[tok(3 * CONV_WIDTH + 2 * RNN_WIDTH), tok(CONV_K - 1, CONV_WIDTH),
                  tok(RNN_CONV_K - 1, RNN_WIDTH), tok(RNN_WIDTH),
                  _resident((CONV_K, CONV_WIDTH)), _resident((RNN_CONV_K, RNN_WIDTH)), row,
                  _resident((RNN_HEADS, RNN_HEAD_DIM, 2 * RNN_HEAD_DIM)), row, row, row],
        out_specs=[tok(CONV_WIDTH), tok(RNN_WIDTH),
                   tok(CONV_K - 1, CONV_WIDTH), tok(RNN_CONV_K - 1, RNN_WIDTH), tok(RNN_WIDTH)],
        out_shape=[jax.ShapeDtypeStruct((n, CONV_WIDTH), BF16), jax.ShapeDtypeStruct((n, RNN_WIDTH), BF16),
                   jax.ShapeDtypeStruct((n, CONV_K - 1, CONV_WIDTH), F32),
                   jax.ShapeDtypeStruct((n, RNN_CONV_K - 1, RNN_WIDTH), F32),
                   jax.ShapeDtypeStruct((n, RNN_WIDTH), F32)],
        compiler_params=_params("arbitrary"),
        name="sample_seq",
    )(z, cst, rst, h0, ccw, rcw, rcb, wg, ba, bx, lam)


def _sample_attn_kernel(q_ref, k_ref, v_ref, y_ref):
    q = q_ref[...]
    sc = jnp.sum(k_ref[0] * q[:, None], axis=-1, keepdims=True) * (ATTN_HEAD_DIM ** -0.5)
    e = jnp.exp(sc - jnp.max(sc, axis=1, keepdims=True))
    pr = e / jnp.sum(e, axis=1, keepdims=True)
    y_ref[...] = jnp.sum(pr * v_ref[0], axis=1).astype(BF16)


def _sample_attn(q, ck, cv, tb=2):
    n = q.shape[0]
    tok = pl.BlockSpec((tb, ATTN_HEADS, ATTN_HEAD_DIM), lambda i: (i, 0, 0))
    mem = pl.BlockSpec((1, tb, N_MEM, ATTN_HEADS, ATTN_HEAD_DIM), lambda i: (0, i, 0, 0, 0))
    return pl.pallas_call(
        _sample_attn_kernel,
        grid=(n // tb,),
        in_specs=[tok, mem, mem],
        out_specs=tok,
        out_shape=jax.ShapeDtypeStruct((n, ATTN_HEADS, ATTN_HEAD_DIM), BF16),
        compiler_params=_params("arbitrary"),
        name="sample_attn",
    )(q, ck, cv)


def _proj_kernel(x_ref, w_ref, o_ref):
    o_ref[...] = _dot(x_ref[...].astype(BF16), w_ref[...])


def _proj(x, w, tn=1024):
    m, k = x.shape
    n = w.shape[1]
    return pl.pallas_call(
        _proj_kernel,
        grid=(n // tn,),
        in_specs=[_resident((m, k)), pl.BlockSpec((k, tn), lambda j: (0, j))],
        out_specs=pl.BlockSpec((m, tn), lambda j: (0, j)),
        out_shape=jax.ShapeDtypeStruct((m, n), F32),
        compiler_params=_params("arbitrary"),
        name="sample_proj",
    )(x, w)


def _merge_kernel(x_ref, yc_ref, yr_ref, ya_ref, wg0_ref, wg1_ref, wg2_ref, wc_ref, wr_ref, wa_ref, wo_ref,
                  g_ref, b_ref, o_ref, xb_ref, acc_ref):
    n = pl.program_id(1)

    @pl.when(n == 0)
    def _():
        xb_ref[...] = x_ref[...].astype(BF16)
        acc_ref[...] = jnp.zeros_like(acc_ref)

    xb = xb_ref[...]
    merged = (_sigmoid(_dot(xb, wg0_ref[...])) * _dot(yc_ref[...], wc_ref[...])
              + _sigmoid(_dot(xb, wg1_ref[...])) * _dot(yr_ref[...], wr_ref[...])
              + _sigmoid(_dot(xb, wg2_ref[...])) * _dot(ya_ref[...], wa_ref[...]))
    acc_ref[...] += _dot(merged.astype(BF16), wo_ref[...])

    @pl.when(n == pl.num_programs(1) - 1)
    def _():
        o_ref[...] = _layernorm(DN_ALPHA * x_ref[...] + acc_ref[...], g_ref[...], b_ref[...])


def _merge(x, yc, yr, ya, wgate, wc, wr, wa, wo, g, b, tm, tn=256):
    t = x.shape[0]
    nb = D_MODEL // tn
    tok_f = pl.BlockSpec((tm, D_MODEL), lambda i, n: (i, 0))
    tok_h = pl.BlockSpec((tm, CONV_WIDTH), lambda i, n: (i, 0))
    gate = [pl.BlockSpec((D_MODEL, tn), functools.partial(lambda i, n, k: (0, k * nb + n), k=k))
            for k in range(N_BRANCH)]
    br = pl.BlockSpec((CONV_WIDTH, tn), lambda i, n: (0, n))
    row = _resident((1, D_MODEL))
    return pl.pallas_call(
        _merge_kernel,
        grid=(t // tm, nb),
        in_specs=[tok_f, tok_h, tok_h, tok_h] + gate + [br, br, br,
                  pl.BlockSpec((tn, D_MODEL), lambda i, n: (n, 0)), row, row],
        out_specs=tok_f,
        out_shape=jax.ShapeDtypeStruct((t, D_MODEL), F32),
        scratch_shapes=[pltpu.VMEM((tm, D_MODEL), BF16), pltpu.VMEM((tm, D_MODEL), F32)],
        compiler_params=_params("arbitrary", "arbitrary"),
        name="merge",
    )(x, yc, yr, ya, wgate, wgate, wgate, wc, wr, wa, wo, g, b)


def _top_rows(sc, k):
    n = sc.shape[0]
    rows = lax.broadcasted_iota(jnp.int32, sc.shape, 0)
    rank = jnp.full(sc.shape, NOT_SELECTED_RANK, F32)
    rem = sc
    vals = []
    for r in range(k):
        m = jnp.max(rem, axis=0, keepdims=True)
        first = jnp.min(jnp.where(rem == m, rows, n), axis=0, keepdims=True)
        hit = rows == first
        rank = jnp.where(hit, float(r), rank)
        rem = jnp.where(hit, -jnp.inf, rem)
        vals.append(m)
    return jnp.concatenate(vals, axis=0), rank


def _top_rows_fast(sc, k):
    rank = jnp.full(sc.shape, NOT_SELECTED_RANK, F32)
    rem = sc
    vals = []
    for r in range(k):
        m = jnp.max(rem, axis=0, keepdims=True)
        hit = rem == m
        rank = jnp.where(hit, float(r), rank)
        rem = jnp.where(hit, -jnp.inf, rem)
        vals.append(m)
    ranked = jnp.sum(jnp.where(rank < NOT_SELECTED_RANK, 1.0, 0.0), axis=0, keepdims=True)
    repeated = jnp.max(ranked) > float(k)
    return lax.cond(repeated, lambda: _top_rows(sc, k), lambda: (jnp.concatenate(vals, axis=0), rank))


_WIDE_R1 = 8


def _peer_sel_kernel(x_ref, wq_ref, keys_ref, r2_ref, b2_ref, n1_ref, a1_ref):
    q = _dot(x_ref[...].astype(BF16), wq_ref[...]).astype(BF16)
    ts = q.shape[0]
    for h in range(PEER_HEADS):
        s1 = _dot_nt(keys_ref[2 * h], q[:, (2 * h) * PEER_HALF:(2 * h + 1) * PEER_HALF])
        s2 = _dot_nt(keys_ref[2 * h + 1], q[:, (2 * h + 1) * PEER_HALF:(2 * h + 2) * PEER_HALF])
        v1, rank1 = _top_rows_fast(s1, PEER_TOPK)
        v2, rank2 = _top_rows_fast(s2, PEER_TOPK)
        blocks = [v1[0:1] + v2]
        blocks += [v1[r:r + 1] + v2[:SUBLANES] for r in range(1, _WIDE_R1)]
        blocks.append(v1[_WIDE_R1:] + v2[0:1])
        cand = jnp.concatenate(blocks, axis=0)
        _, crank = _top_rows(cand, PEER_TOPK)
        chosen = crank < NOT_SELECTED_RANK
        z = jnp.sum(jnp.where(chosen, jnp.exp(cand - cand[0:1]), 0.0), axis=0, keepdims=True)
        cnt = jnp.where(chosen, 1.0, 0.0)
        counts = [jnp.sum(cnt[0:PEER_TOPK], axis=0, keepdims=True)]
        for r in range(1, _WIDE_R1):
            lo = PEER_TOPK + (r - 1) * SUBLANES
            counts.append(jnp.sum(cnt[lo:lo + SUBLANES], axis=0, keepdims=True))
        lo = PEER_TOPK + (_WIDE_R1 - 1) * SUBLANES
        counts += [cnt[lo + r:lo + r + 1] for r in range(PEER_TOPK - _WIDE_R1)]
        n1 = jnp.zeros((PEER_NKEYS, ts), F32)
        for r in range(PEER_TOPK):
            n1 = jnp.where(rank1 == float(r), counts[r], n1)
        r2_ref[h] = rank2.astype(BF16)
        n1_ref[h] = n1
        a1_ref[h] = jnp.exp(s1 - v1[0:1])
        b2_ref[h] = (jnp.exp(s2 - v2[0:1]) / z).astype(BF16)


def _peer_sel(x1, wq, keys, ts):
    t = x1.shape[0]
    sel = pl.BlockSpec((PEER_HEADS, PEER_NKEYS, ts), lambda i: (0, 0, i))
    shp = lambda dt: jax.ShapeDtypeStruct((PEER_HEADS, PEER_NKEYS, t), dt)
    r2, b2, n1, a1 = pl.pallas_call(
        _peer_sel_kernel,
        grid=(t // ts,),
        in_specs=[pl.BlockSpec((ts, D_MODEL), lambda i: (i, 0)),
                  _resident((D_MODEL, PEER_HEADS * 2 * PEER_HALF)),
                  _resident((PEER_HEADS * 2, PEER_NKEYS, PEER_HALF))],
        out_specs=[sel] * 4,
        out_shape=[shp(BF16), shp(BF16), shp(F32), shp(F32)],
        compiler_params=_params("arbitrary"),
        name="peer_sel",
    )(x1, wq, keys)
    grouped = (PEER_HEADS, PEER_NKEYS // BF16_ROWS, BF16_ROWS, t)
    return r2.reshape(grouped), b2.reshape(grouped), n1, a1


def _expert_weights(r2_ref, b2_ref, n1_ref, a1_ref, ii, tm):
    w = None
    for h in range(PEER_HEADS):
        nrow = jnp.broadcast_to(n1_ref[h, ii:ii + 1, :], (BF16_ROWS, tm)).astype(BF16)
        arow = jnp.broadcast_to(a1_ref[h, ii:ii + 1, :], (BF16_ROWS, tm)).astype(BF16)
        wh = jnp.where(r2_ref[h] < nrow[None], b2_ref[h], jnp.zeros((), BF16)) * arow[None]
        w = wh if w is None else w + wh
    return w.reshape(PEER_NKEYS, tm)


def _peer_ffn_kernel(x_ref, r2_ref, b2_ref, n1_ref, a1_ref, u_ref, vt_ref, g_ref, b_ref, o_ref,
                     xt_ref, act_ref, pt_ref, acc_ref, *, rows_per_chunk):
    c = pl.program_id(1)
    tm = x_ref.shape[0]
    piece = 2 * PEER_NKEYS

    @pl.when(c == 0)
    def _():
        xt_ref[...] = x_ref[...].T.astype(BF16)
        acc_ref[...] = jnp.zeros_like(acc_ref)

    def project(p):
        act_ref[p * piece:(p + 1) * piece, :] = _dot(u_ref[p * piece:(p + 1) * piece, :], xt_ref[...])

    def weigh(p):
        for ii in (2 * p, 2 * p + 1):
            rows = slice(ii * PEER_NKEYS, (ii + 1) * PEER_NKEYS)
            w = _expert_weights(r2_ref, b2_ref, n1_ref, a1_ref, ii, tm)
            pt_ref[rows, :] = w * _gelu(act_ref[rows, :]).astype(BF16)

    project(0)
    for p in range(1, rows_per_chunk // 2):
        project(p)
        weigh(p - 1)
    weigh(rows_per_chunk // 2 - 1)
    acc_ref[...] += _dot(vt_ref[...], pt_ref[...])

    @pl.when(c == pl.num_programs(1) - 1)
    def _():
        o_ref[...] = _layernorm(DN_ALPHA * x_ref[...] + acc_ref[...].T, g_ref[...], b_ref[...])


def _peer_ffn(x1, r2, b2, n1, a1, u, vt, g, b, tm, ec=1024):
    t = x1.shape[0]
    rows_per_chunk = ec // PEER_NKEYS
    tok = pl.BlockSpec((tm, D_MODEL), lambda i, c: (i, 0))
    col = pl.BlockSpec((PEER_HEADS, PEER_NKEYS // BF16_ROWS, BF16_ROWS, tm), lambda i, c: (0, 0, 0, i))
    chunk_rows = pl.BlockSpec((PEER_HEADS, rows_per_chunk, tm), lambda i, c: (0, c, i))
    row = _resident((1, D_MODEL))
    return pl.pallas_call(
        functools.partial(_peer_ffn_kernel, rows_per_chunk=rows_per_chunk),
        grid=(t // tm, PEER_EXPERTS // ec),
        in_specs=[tok, col, col, chunk_rows, chunk_rows, pl.BlockSpec((ec, D_MODEL), lambda i, c: (c, 0)),
                  pl.BlockSpec((D_MODEL, ec), lambda i, c: (0, c)), row, row],
        out_specs=tok,
        out_shape=jax.ShapeDtypeStruct((t, D_MODEL), F32),
        scratch_shapes=[pltpu.VMEM((D_MODEL, tm), BF16), pltpu.VMEM((ec, tm), F32), pltpu.VMEM((ec, tm), BF16),
                        pltpu.VMEM((D_MODEL, tm), F32)],
        compiler_params=_params("arbitrary", "arbitrary"),
        name="peer_ffn",
    )(x1, r2, b2, n1, a1, u, vt, g, b)


def _channel_sublayers(x, yc, yr, ya, wts, tm):
    x1 = _merge(x, yc, yr, ya, wts["gate"], wts["br_conv"], wts["br_rnn"], wts["br_attn"], wts["o"],
                wts["ln1_g"], wts["ln1_b"], tm)
    r2, b2, n1, a1 = _peer_sel(x1, wts["peer_wq"], wts["peer_keys"], min(tm, 256))
    return _peer_ffn(x1, r2, b2, n1, a1, wts["peer_u"], wts["peer_vt"], wts["ln2_g"], wts["ln2_b"], tm)


def kernel(x_prompt, x_sample, mem_prompt, cache_mem_k, cache_mem_v, state_conv_z, state_rglru_conv, state_rglru_h, w_in, conv_w, rg_conv_w, rg_conv_b, rg_wa, rg_ba, rg_wx, rg_bx, rg_lambda, w_mk, w_mv, w_br_conv, w_br_rnn, w_br_attn, w_o, ln1_g, ln1_b, peer_wq, peer_keys, peer_u, peer_v, ln2_g, ln2_b):
    assert w_in.shape[0] == DEPTH == 1
    bp, sp, _ = x_prompt.shape
    bs = x_sample.shape[0]
    l = 0
    c3 = 3 * CONV_WIDTH
    r2w = 2 * RNN_WIDTH
    win = w_in[l].astype(BF16)
    w_conv, w_rnn = win[:, :c3], win[:, c3:c3 + r2w]
    w_q = win[:, c3 + r2w:c3 + r2w + ATTN_WIDTH]
    row = lambda a: a[l].reshape(1, -1)
    wg = jnp.concatenate([rg_wa[l], rg_wx[l]], axis=-1).astype(BF16)
    wts = {
        "gate": win[:, c3 + r2w + ATTN_WIDTH:],
        "br_conv": w_br_conv[l].astype(BF16), "br_rnn": w_br_rnn[l].astype(BF16),
        "br_attn": w_br_attn[l].astype(BF16), "o": w_o[l].astype(BF16),
        "ln1_g": row(ln1_g), "ln1_b": row(ln1_b), "ln2_g": row(ln2_g), "ln2_b": row(ln2_b),
        "peer_wq": peer_wq[l].astype(BF16),
        "peer_keys": peer_keys[l].reshape(PEER_HEADS * 2, PEER_NKEYS, PEER_HALF).astype(BF16),
        "peer_u": peer_u[l].astype(BF16), "peer_vt": peer_v[l].astype(BF16).T,
    }
    rnn_args = (rg_conv_w[l], row(rg_conv_b), wg, row(rg_ba), row(rg_bx), row(rg_lambda))

    mk, mv = _mem_kv(mem_prompt.reshape(bp * N_MEM, D_MODEL), w_mk[l].astype(BF16), w_mv[l].astype(BF16))
    yc_p, cz_p = _conv_branch(x_prompt, w_conv, conv_w[l])
    yr_p, rc_p, h_p = _rnn_branch(x_prompt, w_rnn, *rnn_args)
    ya_p = _attn_branch(x_prompt, w_q, mk.reshape(bp, N_MEM, ATTN_WIDTH), mv.reshape(bp, N_MEM, ATTN_WIDTH))
    tp = bp * sp
    y_p = _channel_sublayers(x_prompt.reshape(tp, D_MODEL), yc_p.reshape(tp, CONV_WIDTH),
                             yr_p.reshape(tp, RNN_WIDTH), ya_p.reshape(tp, ATTN_WIDTH), wts, 512)

    xs = x_sample.reshape(bs, D_MODEL)
    z_s = _proj(xs, win[:, :c3 + r2w + ATTN_WIDTH])
    yc_s, yr_s, cz_s, rc_s, h_s = _sample_seq(
        z_s[:, :c3 + r2w], state_conv_z[l], state_rglru_conv[l], state_rglru_h[l], conv_w[l], *rnn_args)
    ya_s = _sample_attn(z_s[:, c3 + r2w:].reshape(bs, ATTN_HEADS, ATTN_HEAD_DIM), cache_mem_k, cache_mem_v)
    ya_s = ya_s.reshape(bs, ATTN_WIDTH)
    y_s = _channel_sublayers(xs, yc_s, yr_s, ya_s, wts, bs)

    hd = (ATTN_HEADS, ATTN_HEAD_DIM)
    return (y_p.reshape(bp, sp, D_MODEL), y_s.reshape(bs, 1, D_MODEL),
            mk.reshape(1, bp, N_MEM, *hd), mv.reshape(1, bp, N_MEM, *hd),
            cz_p[None], rc_p[None], h_p.reshape(1, bp, RNN_WIDTH),
            cz_s[None], rc_s[None], h_s[None])
```

```python
import functools

import jax
import jax.numpy as jnp
from jax import lax
from jax.experimental import pallas as pl
from jax.experimental.pallas import tpu as pltpu

F32 = jnp.float32
BF16 = jnp.bfloat16

D_MODEL = 2048
N_MEM = 256
CONV_WIDTH = 1024
CONV_K = 3
RNN_WIDTH = 1024
RNN_HEADS = 8
RNN_HEAD_DIM = RNN_WIDTH // RNN_HEADS
RNN_CONV_K = 4
RG_C = 8.0
ATTN_HEADS = 4
ATTN_HEAD_DIM = 256
ATTN_WIDTH = ATTN_HEADS * ATTN_HEAD_DIM
N_BRANCH = 3
PEER_HEADS = 8
PEER_NKEYS = 128
PEER_EXPERTS = PEER_NKEYS * PEER_NKEYS
PEER_TOPK = 16
PEER_HALF = 128
DEPTH = 1
DN_ALPHA = (2.0 * DEPTH) ** 0.25
LN_EPS = 1e-5

VMEM_LIMIT_V7X = 56 * 1024 * 1024
SUBLANES = 8
LANES = 128
BF16_ROWS = 2 * SUBLANES
NOT_SELECTED_RANK = 99.0


def _params(*sem):
    return pltpu.CompilerParams(dimension_semantics=sem, vmem_limit_bytes=VMEM_LIMIT_V7X)


def _resident(shape):
    n = len(shape)
    return pl.BlockSpec(shape, lambda *_: (0,) * n, pipeline_mode=pl.Buffered(1))


def _dot(a, b):
    return jnp.dot(a, b, preferred_element_type=F32)


def _dot_nt(a, b):
    return lax.dot_general(a, b, (((1,), (1,)), ((), ())), preferred_element_type=F32)


def _dot_tn(a, b):
    return lax.dot_general(a, b, (((0,), (0,)), ((), ())), preferred_element_type=F32)


def _sigmoid(x):
    return 1.0 / (1.0 + jnp.exp(-x))


def _gelu(x):
    return 0.5 * x * (1.0 + jnp.tanh(0.7978845608028654 * (x + 0.044715 * (x * x * x))))


def _softplus(x):
    return jnp.maximum(x, 0.0) + jnp.log1p(jnp.exp(-jnp.abs(x)))


def _layernorm(y, g, b):
    mu = jnp.mean(y, axis=-1, keepdims=True)
    yc = y - mu
    var = jnp.mean(yc * yc, axis=-1, keepdims=True)
    return yc * lax.rsqrt(var + LN_EPS) * g + b


def _mem_kv_kernel(m_ref, wk_ref, wv_ref, k_ref, v_ref):
    mb = m_ref[...].astype(BF16)
    k_ref[...] = _dot(mb, wk_ref[...])
    v_ref[...] = _dot(mb, wv_ref[...])


def _mem_kv(mem, wk, wv, tm=256):
    m = mem.shape[0]
    return pl.pallas_call(
        _mem_kv_kernel,
        grid=(m // tm,),
        in_specs=[pl.BlockSpec((tm, D_MODEL), lambda i: (i, 0)),
                  _resident((D_MODEL, ATTN_WIDTH)), _resident((D_MODEL, ATTN_WIDTH))],
        out_specs=[pl.BlockSpec((tm, ATTN_WIDTH), lambda i: (i, 0))] * 2,
        out_shape=[jax.ShapeDtypeStruct((m, ATTN_WIDTH), F32)] * 2,
        compiler_params=_params("arbitrary"),
        name="mem_kv",
    )(mem, wk, wv)


def _conv_kernel(x_ref, w_ref, cw_ref, y_ref, st_ref, pb_ref, *, ts):
    @pl.when(pl.program_id(1) == 0)
    def _():
        pb_ref[0:SUBLANES, :] = jnp.zeros((SUBLANES, CONV_WIDTH), F32)

    z = _dot(x_ref[0].astype(BF16), w_ref[...])
    p = z[:, CONV_WIDTH:2 * CONV_WIDTH] * z[:, 2 * CONV_WIDTH:]
    pb_ref[SUBLANES:SUBLANES + ts, :] = p
    p1 = pb_ref[SUBLANES - 1:SUBLANES - 1 + ts, :]
    p2 = pb_ref[SUBLANES - 2:SUBLANES - 2 + ts, :]
    cw = cw_ref[...]
    y = z[:, :CONV_WIDTH] * (cw[0:1] * p2 + cw[1:2] * p1 + cw[2:3] * p)
    y_ref[0] = y.astype(BF16)
    tail = pb_ref[ts:ts + SUBLANES, :]
    st_ref[0] = tail[SUBLANES - (CONV_K - 1):]
    pb_ref[0:SUBLANES, :] = tail


def _conv_branch(x, w, cw, ts=512):
    b, s, _ = x.shape
    return pl.pallas_call(
        functools.partial(_conv_kernel, ts=ts),
        grid=(b, s // ts),
        in_specs=[pl.BlockSpec((1, ts, D_MODEL), lambda i, j: (i, j, 0)),
                  _resident((D_MODEL, 3 * CONV_WIDTH)), _resident((CONV_K, CONV_WIDTH))],
        out_specs=[pl.BlockSpec((1, ts, CONV_WIDTH), lambda i, j: (i, j, 0)),
                   pl.BlockSpec((1, CONV_K - 1, CONV_WIDTH), lambda i, j: (i, 0, 0))],
        out_shape=[jax.ShapeDtypeStruct((b, s, CONV_WIDTH), BF16),
                   jax.ShapeDtypeStruct((b, CONV_K - 1, CONV_WIDTH), F32)],
        scratch_shapes=[pltpu.VMEM((SUBLANES + ts, CONV_WIDTH), F32)],
        compiler_params=_params("arbitrary", "arbitrary"),
        name="conv_branch",
    )(x, w, cw)


def _rglru_gates(xr, wg_ref, ba, bx, lam):
    xb = xr.astype(BF16)
    gr, gi = [], []
    for h in range(RNN_HEADS):
        g = _dot(xb[:, h * RNN_HEAD_DIM:(h + 1) * RNN_HEAD_DIM], wg_ref[h])
        gr.append(g[:, :RNN_HEAD_DIM])
        gi.append(g[:, RNN_HEAD_DIM:])
    r = _sigmoid(jnp.concatenate(gr, axis=1) + ba)
    i = _sigmoid(jnp.concatenate(gi, axis=1) + bx)
    log_a = -RG_C * r * _softplus(-lam)
    a = jnp.exp(log_a)
    th = jnp.tanh(log_a)
    u = jnp.sqrt(-2.0 * th / (1.0 - th)) * (i * xr)
    return a, u


def _prefix_scan(a, u, ts):
    rows = lax.broadcasted_iota(jnp.int32, a.shape, 0)
    d = 1
    while d < ts:
        keep = rows >= d
        u = jnp.where(keep, u + a * pltpu.roll(u, d, axis=0), u)
        a = jnp.where(keep, a * pltpu.roll(a, d, axis=0), a)
        d *= 2
    return a, u


def _rnn_kernel(x_ref, w_ref, cw_ref, cb_ref, wg_ref, ba_ref, bx_ref, lam_ref,
                y_ref, buf_ref, h_ref, rb_ref, hc_ref, *, ts):
    @pl.when(pl.program_id(1) == 0)
    def _():
        rb_ref[0:SUBLANES, :] = jnp.zeros((SUBLANES, RNN_WIDTH), F32)
        hc_ref[...] = jnp.zeros((1, RNN_WIDTH), F32)

    z = _dot(x_ref[0].astype(BF16), w_ref[...])
    rx = z[:, :RNN_WIDTH]
    rb_ref[SUBLANES:SUBLANES + ts, :] = rx
    cw = cw_ref[...]
    xr = cw[3:4] * rx + cb_ref[...]
    for k in range(RNN_CONV_K - 1):
        off = SUBLANES - (RNN_CONV_K - 1) + k
        xr = xr + cw[k:k + 1] * rb_ref[off:off + ts, :]
    a, u = _rglru_gates(xr, wg_ref, ba_ref[...], bx_ref[...], lam_ref[...])
    a_pre, h0 = _prefix_scan(a, u, ts)
    h = a_pre * hc_ref[...] + h0
    y_ref[0] = (h * _gelu(z[:, RNN_WIDTH:])).astype(BF16)
    hc_ref[...] = h[ts - 1:ts, :]
    h_ref[0] = h[ts - 1:ts, :]
    tail = rb_ref[ts:ts + SUBLANES, :]
    buf_ref[0] = tail[SUBLANES - (RNN_CONV_K - 1):]
    rb_ref[0:SUBLANES, :] = tail


def _rnn_branch(x, w, cw, cb, wg, ba, bx, lam, ts=256):
    b, s, _ = x.shape
    row = _resident((1, RNN_WIDTH))
    return pl.pallas_call(
        functools.partial(_rnn_kernel, ts=ts),
        grid=(b, s // ts),
        in_specs=[pl.BlockSpec((1, ts, D_MODEL), lambda i, j: (i, j, 0)),
                  _resident((D_MODEL, 2 * RNN_WIDTH)), _resident((RNN_CONV_K, RNN_WIDTH)), row,
                  _resident((RNN_HEADS, RNN_HEAD_DIM, 2 * RNN_HEAD_DIM)), row, row, row],
        out_specs=[pl.BlockSpec((1, ts, RNN_WIDTH), lambda i, j: (i, j, 0)),
                   pl.BlockSpec((1, RNN_CONV_K - 1, RNN_WIDTH), lambda i, j: (i, 0, 0)),
                   pl.BlockSpec((1, 1, RNN_WIDTH), lambda i, j: (i, 0, 0))],
        out_shape=[jax.ShapeDtypeStruct((b, s, RNN_WIDTH), BF16),
                   jax.ShapeDtypeStruct((b, RNN_CONV_K - 1, RNN_WIDTH), F32),
                   jax.ShapeDtypeStruct((b, 1, RNN_WIDTH), F32)],
        scratch_shapes=[pltpu.VMEM((SUBLANES + ts, RNN_WIDTH), F32), pltpu.VMEM((1, RNN_WIDTH), F32)],
        compiler_params=_params("arbitrary", "arbitrary"),
        name="rnn_branch",
    )(x, w, cw, cb, wg, ba, bx, lam)


def _attn_kernel(x_ref, w_ref, k_ref, v_ref, y_ref):
    q = _dot(x_ref[0].astype(BF16), w_ref[...]).astype(BF16)
    kb = k_ref[0].astype(BF16)
    vb = v_ref[0].astype(BF16)
    outs = []
    for h in range(ATTN_HEADS):
        sl = slice(h * ATTN_HEAD_DIM, (h + 1) * ATTN_HEAD_DIM)
        sc = _dot_nt(q[:, sl], kb[:, sl]) * (ATTN_HEAD_DIM ** -0.5)
        e = jnp.exp(sc - jnp.max(sc, axis=-1, keepdims=True))
        p = e / jnp.sum(e, axis=-1, keepdims=True)
        outs.append(_dot(p.astype(BF16), vb[:, sl]))
    y_ref[0] = jnp.concatenate(outs, axis=1).astype(BF16)


def _attn_branch(x, w, mk, mv, ts=512):
    b, s, _ = x.shape
    return pl.pallas_call(
        _attn_kernel,
        grid=(b, s // ts),
        in_specs=[pl.BlockSpec((1, ts, D_MODEL), lambda i, j: (i, j, 0)),
                  _resident((D_MODEL, ATTN_WIDTH)),
                  pl.BlockSpec((1, N_MEM, ATTN_WIDTH), lambda i, j: (i, 0, 0)),
                  pl.BlockSpec((1, N_MEM, ATTN_WIDTH), lambda i, j: (i, 0, 0))],
        out_specs=pl.BlockSpec((1, ts, ATTN_WIDTH), lambda i, j: (i, j, 0)),
        out_shape=jax.ShapeDtypeStruct((b, s, ATTN_WIDTH), BF16),
        compiler_params=_params("arbitrary", "arbitrary"),
        name="attn_branch",
    )(x, w, mk, mv)


def _sample_seq_kernel(z_ref, cst_ref, rst_ref, h0_ref, ccw_ref, rcw_ref, rcb_ref, wg_ref,
                       ba_ref, bx_ref, lam_ref, yc_ref, yr_ref, cz_ref, rc_ref, h_ref):
    z = z_ref[...]
    p = z[:, CONV_WIDTH:2 * CONV_WIDTH] * z[:, 2 * CONV_WIDTH:3 * CONV_WIDTH]
    cst = cst_ref[...]
    ccw = ccw_ref[...]
    conv_y = ccw[0:1] * cst[:, 0, :] + ccw[1:2] * cst[:, 1, :] + ccw[2:3] * p
    yc_ref[...] = (z[:, :CONV_WIDTH] * conv_y).astype(BF16)
    cz_ref[:, 0, :] = cst[:, 1, :]
    cz_ref[:, 1, :] = p
    rx = z[:, 3 * CONV_WIDTH:3 * CONV_WIDTH + RNN_WIDTH]
    rst = rst_ref[...]
    rcw = rcw_ref[...]
    xr = rcw[0:1] * rst[:, 0, :] + rcw[1:2] * rst[:, 1, :] + rcw[2:3] * rst[:, 2, :] + rcw[3:4] * rx + rcb_ref[...]
    a, u = _rglru_gates(xr, wg_ref, ba_ref[...], bx_ref[...], lam_ref[...])
    h = a * h0_ref[...] + u
    yr_ref[...] = (h * _gelu(z[:, 3 * CONV_WIDTH + RNN_WIDTH:])).astype(BF16)
    h_ref[...] = h
    rc_ref[:, 0, :] = rst[:, 1, :]
    rc_ref[:, 1, :] = rst[:, 2, :]
    rc_ref[:, 2, :] = rx


def _sample_seq(z, cst, rst, h0, ccw, rcw, rcb, wg, ba, bx, lam, tb=32):
    n = z.shape[0]
    row = _resident((1, RNN_WIDTH))

    def tok(*tail):
        nd = len(tail)
        return pl.BlockSpec((tb,) + tail, lambda i: (i,) + (0,) * nd)

    return pl.pallas_call(
        _sample_seq_kernel,
        grid=(n // tb,),
        in_specs=[tok(3 * CONV_WIDTH + 2 * RNN_WIDTH), tok(CONV_K - 1, CONV_WIDTH),
                  tok(RNN_CONV_K - 1, RNN_WIDTH), tok(RNN_WIDTH),
                  _resident((CONV_K, CONV_WIDTH)), _resident((RNN_CONV_K, RNN_WIDTH)), row,
                  _resident((RNN_HEADS, RNN_HEAD_DIM, 2 * RNN_HEAD_DIM)), row, row, row],
        out_specs=[tok(CONV_WIDTH), tok(RNN_WIDTH),
                   tok(CONV_K - 1, CONV_WIDTH), tok(RNN_CONV_K - 1, RNN_WIDTH), tok(RNN_WIDTH)],
        out_shape=[jax.ShapeDtypeStruct((n, CONV_WIDTH), BF16), jax.ShapeDtypeStruct((n, RNN_WIDTH), BF16),
                   jax.ShapeDtypeStruct((n, CONV_K - 1, CONV_WIDTH), F32),
                   jax.ShapeDtypeStruct((n, RNN_CONV_K - 1, RNN_WIDTH), F32),
                   jax.ShapeDtypeStruct((n, RNN_WIDTH), F32)],
        compiler_params=_params("arbitrary"),
        name="sample_seq",
    )(z, cst, rst, h0, ccw, rcw, rcb, wg, ba, bx, lam)


def _sample_attn_kernel(q_ref, k_ref, v_ref, y_ref):
    q = q_ref[...]
    sc = jnp.sum(k_ref[0] * q[:, None], axis=-1, keepdims=True) * (ATTN_HEAD_DIM ** -0.5)
    e = jnp.exp(sc - jnp.max(sc, axis=1, keepdims=True))
    pr = e / jnp.sum(e, axis=1, keepdims=True)
    y_ref[...] = jnp.sum(pr * v_ref[0], axis=1).astype(BF16)


def _sample_attn(q, ck, cv, tb=2):
    n = q.shape[0]
    tok = pl.BlockSpec((tb, ATTN_HEADS, ATTN_HEAD_DIM), lambda i: (i, 0, 0))
    mem = pl.BlockSpec((1, tb, N_MEM, ATTN_HEADS, ATTN_HEAD_DIM), lambda i: (0, i, 0, 0, 0))
    return pl.pallas_call(
        _sample_attn_kernel,
        grid=(n // tb,),
        in_specs=[tok, mem, mem],
        out_specs=tok,
        out_shape=jax.ShapeDtypeStruct((n, ATTN_HEADS, ATTN_HEAD_DIM), BF16),
        compiler_params=_params("arbitrary"),
        name="sample_attn",
    )(q, ck, cv)


def _proj_kernel(x_ref, w_ref, o_ref):
    o_ref[...] = _dot(x_ref[...].astype(BF16), w_ref[...])


def _proj(x, w, tn=1024):
    m, k = x.shape
    n = w.shape[1]
    return pl.pallas_call(
        _proj_kernel,
        grid=(n // tn,),
        in_specs=[_resident((m, k)), pl.BlockSpec((k, tn), lambda j: (0, j))],
        out_specs=pl.BlockSpec((m, tn), lambda j: (0, j)),
        out_shape=jax.ShapeDtypeStruct((m, n), F32),
        compiler_params=_params("arbitrary"),
        name="sample_proj",
    )(x, w)


def _merge_kernel(x_ref, yc_ref, yr_ref, ya_ref, wg0_ref, wg1_ref, wg2_ref, wc_ref, wr_ref, wa_ref, wo_ref,
                  g_ref, b_ref, o_ref, xb_ref):
    n = pl.program_id(1)

    @pl.when(n == 0)
    def _():
        xb_ref[...] = x_ref[...].astype(BF16)
        o_ref[...] = jnp.zeros_like(o_ref)

    xb = xb_ref[...]
    merged = (_sigmoid(_dot(xb, wg0_ref[...])) * _dot(yc_ref[...], wc_ref[...])
              + _sigmoid(_dot(xb, wg1_ref[...])) * _dot(yr_ref[...], wr_ref[...])
              + _sigmoid(_dot(xb, wg2_ref[...])) * _dot(ya_ref[...], wa_ref[...]))
    o_ref[...] += _dot(merged.astype(BF16), wo_ref[...])

    @pl.when(n == pl.num_programs(1) - 1)
    def _():
        o_ref[...] = _layernorm(DN_ALPHA * x_ref[...] + o_ref[...], g_ref[...], b_ref[...])


def _merge(x, yc, yr, ya, wgate, wc, wr, wa, wo, g, b, tm, tn=512):
    t = x.shape[0]
    nb = D_MODEL // tn
    tok_f = pl.BlockSpec((tm, D_MODEL), lambda i, n: (i, 0))
    tok_x = pl.BlockSpec((tm, D_MODEL), lambda i, n: (i, 0), pipeline_mode=pl.Buffered(1))
    tok_h = pl.BlockSpec((tm, CONV_WIDTH), lambda i, n: (i, 0))
    gate = [pl.BlockSpec((D_MODEL, tn), functools.partial(lambda i, n, k: (0, k * nb + n), k=k))
            for k in range(N_BRANCH)]
    br = pl.BlockSpec((CONV_WIDTH, tn), lambda i, n: (0, n))
    row = _resident((1, D_MODEL))
    return pl.pallas_call(
        _merge_kernel,
        grid=(t // tm, nb),
        in_specs=[tok_x, tok_h, tok_h, tok_h] + gate + [br, br, br,
                  pl.BlockSpec((tn, D_MODEL), lambda i, n: (n, 0)), row, row],
        out_specs=tok_f,
        out_shape=jax.ShapeDtypeStruct((t, D_MODEL), F32),
        scratch_shapes=[pltpu.VMEM((tm, D_MODEL), BF16)],
        compiler_params=_params("arbitrary", "arbitrary"),
        name="merge",
    )(x, yc, yr, ya, wgate, wgate, wgate, wc, wr, wa, wo, g, b)


def _top_rows(sc, k):
    n = sc.shape[0]
    rows = lax.broadcasted_iota(jnp.int32, sc.shape, 0)
    rank = jnp.full(sc.shape, NOT_SELECTED_RANK, F32)
    rem = sc
    vals = []
    for r in range(k):
        m = jnp.max(rem, axis=0, keepdims=True)
        first = jnp.min(jnp.where(rem == m, rows, n), axis=0, keepdims=True)
        hit = rows == first
        rank = jnp.where(hit, float(r), rank)
        rem = jnp.where(hit, -jnp.inf, rem)
        vals.append(m)
    return jnp.concatenate(vals, axis=0), rank


_WIDE_R1 = 8


def _candidates(v1, v2):
    blocks = [v1[0:1] + v2]
    blocks += [v1[r:r + 1] + v2[:SUBLANES] for r in range(1, _WIDE_R1)]
    blocks.append(v1[_WIDE_R1:] + v2[0:1])
    return jnp.concatenate(blocks, axis=0)


def _row_counts(chosen):
    cnt = jnp.where(chosen, 1.0, 0.0)
    counts = [jnp.sum(cnt[0:PEER_TOPK], axis=0, keepdims=True)]
    for r in range(1, _WIDE_R1):
        lo = PEER_TOPK + (r - 1) * SUBLANES
        counts.append(jnp.sum(cnt[lo:lo + SUBLANES], axis=0, keepdims=True))
    lo = PEER_TOPK + (_WIDE_R1 - 1) * SUBLANES
    counts += [cnt[lo + r:lo + r + 1] for r in range(PEER_TOPK - _WIDE_R1)]
    return counts


def _select_exact(s1, s2):
    v1, rank1 = _top_rows(s1, PEER_TOPK)
    v2, rank2 = _top_rows(s2, PEER_TOPK)
    cand = _candidates(v1, v2)
    _, crank = _top_rows(cand, PEER_TOPK)
    chosen = crank < NOT_SELECTED_RANK
    z = jnp.sum(jnp.where(chosen, jnp.exp(cand - cand[0:1]), 0.0), axis=0, keepdims=True)
    counts = _row_counts(chosen)
    n1 = jnp.zeros(s1.shape, F32)
    for r in range(PEER_TOPK):
        n1 = jnp.where(rank1 == float(r), counts[r], n1)
    return rank2, jnp.exp(s2 - v2[0:1]) / z, n1, jnp.exp(s1 - v1[0:1])


def _sorting_network(lo, hi):
    def merge(lo, hi, r):
        step = r * 2
        if step < hi - lo:
            yield from merge(lo, hi, step)
            yield from merge(lo + r, hi, step)
            yield from ((i, i + r) for i in range(lo + r, hi - r, step))
        else:
            yield (lo, lo + r)

    if hi - lo >= 1:
        mid = lo + (hi - lo) // 2
        yield from _sorting_network(lo, mid)
        yield from _sorting_network(mid + 1, hi)
        yield from merge(lo, hi, 1)


def _top_values(sc, k):
    assert sc.shape[0] == SUBLANES * k
    x = [sc[SUBLANES * v:SUBLANES * (v + 1)] for v in range(k)]
    for a, b in _sorting_network(0, k - 1):
        x[a], x[b] = jnp.maximum(x[a], x[b]), jnp.minimum(x[a], x[b])
    vals = []
    for r in range(k):
        m = jnp.max(x[0], axis=0, keepdims=True)
        vals.append(m)
        live = k - r
        if live > 1:
            hit = x[0] == m
            for p in range(live - 1):
                x[p] = jnp.where(hit, x[p + 1], x[p])
    return vals


def _count_at_least(sc, v):
    return jnp.sum(jnp.where(sc >= v, 1.0, 0.0), axis=0, keepdims=True)


def _select_distinct(s1, s2):
    v1 = _top_values(s1, PEER_TOPK)
    v2 = _top_values(s2, PEER_TOPK)
    cand = _candidates(jnp.concatenate(v1, axis=0), jnp.concatenate(v2, axis=0))
    rem = cand
    for r in range(PEER_TOPK):
        last = jnp.max(rem, axis=0, keepdims=True)
        rem = jnp.where(rem == last, -jnp.inf, rem)
    chosen = rem == -jnp.inf
    most = jnp.maximum(jnp.maximum(_count_at_least(s1, v1[-1]), _count_at_least(s2, v2[-1])),
                       _count_at_least(cand, last))
    z = jnp.sum(jnp.where(chosen, jnp.exp(cand - cand[0:1]), 0.0), axis=0, keepdims=True)
    counts = _row_counts(chosen)
    n1 = jnp.zeros(s1.shape, F32)
    rank2 = jnp.full(s2.shape, NOT_SELECTED_RANK, F32)
    for r in range(PEER_TOPK):
        n1 = jnp.where(s1 == v1[r], counts[r], n1)
        rank2 = jnp.where(s2 == v2[r], float(r), rank2)
    return (rank2, jnp.exp(s2 - v2[0]) / z, n1, jnp.exp(s1 - v1[0])), most


def _peer_sel_kernel(x_ref, wq_ref, keys_ref, r2_ref, b2_ref, n1_ref, a1_ref, s_ref):
    q = _dot(x_ref[...].astype(BF16), wq_ref[...]).astype(BF16)
    ts = q.shape[0]

    def store(h, cols, sel):
        rank2, b2, n1, a1 = sel
        r2_ref[h, :, cols] = rank2.astype(BF16)
        b2_ref[h, :, cols] = b2.astype(BF16)
        n1_ref[h, :, cols] = n1
        a1_ref[h, :, cols] = a1

    for h in range(PEER_HEADS):
        s_ref[0] = _dot_nt(keys_ref[2 * h], q[:, (2 * h) * PEER_HALF:(2 * h + 1) * PEER_HALF])
        s_ref[1] = _dot_nt(keys_ref[2 * h + 1], q[:, (2 * h + 1) * PEER_HALF:(2 * h + 2) * PEER_HALF])
        most = None
        for c in range(ts // LANES):
            cols = slice(c * LANES, (c + 1) * LANES)
            sel, m = _select_distinct(s_ref[0, :, cols], s_ref[1, :, cols])
            store(h, cols, sel)
            most = m if most is None else jnp.maximum(most, m)

        @pl.when(jnp.max(most) > float(PEER_TOPK))
        def _():
            store(h, slice(None), _select_exact(s_ref[0], s_ref[1]))


def _peer_sel(x1, wq, keys, ts):
    t = x1.shape[0]
    sel = pl.BlockSpec((PEER_HEADS, PEER_NKEYS, ts), lambda i: (0, 0, i))
    shp = lambda dt: jax.ShapeDtypeStruct((PEER_HEADS, PEER_NKEYS, t), dt)
    r2, b2, n1, a1 = pl.pallas_call(
        _peer_sel_kernel,
        grid=(t // ts,),
        in_specs=[pl.BlockSpec((ts, D_MODEL), lambda i: (i, 0)),
                  _resident((D_MODEL, PEER_HEADS * 2 * PEER_HALF)),
                  _resident((PEER_HEADS * 2, PEER_NKEYS, PEER_HALF))],
        out_specs=[sel] * 4,
        out_shape=[shp(BF16), shp(BF16), shp(F32), shp(F32)],
        scratch_shapes=[pltpu.VMEM((2, PEER_NKEYS, ts), F32)],
        compiler_params=_params("arbitrary"),
        name="peer_sel",
    )(x1, wq, keys)
    grouped = (PEER_HEADS, PEER_NKEYS // BF16_ROWS, BF16_ROWS, t)
    return r2.reshape(grouped), b2.reshape(grouped), n1, a1


def _expert_weights(r2_ref, b2_ref, n1_ref, a1_ref, ii, tm):
    w = None
    for h in range(PEER_HEADS):
        nrow = jnp.broadcast_to(n1_ref[h, ii:ii + 1, :], (BF16_ROWS, tm)).astype(BF16)
        arow = jnp.broadcast_to(a1_ref[h, ii:ii + 1, :], (BF16_ROWS, tm)).astype(BF16)
        wh = jnp.where(r2_ref[h] < nrow[None], b2_ref[h], jnp.zeros((), BF16)) * arow[None]
        w = wh if w is None else w + wh
    return w.reshape(PEER_NKEYS, tm)


def _peer_ffn_kernel(x_ref, r2_ref, b2_ref, n1_ref, a1_ref, u_ref, vt_ref, g_ref, b_ref, o_ref,
                     xt_ref, act_ref, pt_ref, acc_ref, *, rows_per_chunk):
    c = pl.program_id(1)
    tm = x_ref.shape[0]
    piece = 2 * PEER_NKEYS

    @pl.when(c == 0)
    def _():
        xt_ref[...] = x_ref[...].T.astype(BF16)
        acc_ref[...] = jnp.zeros_like(acc_ref)

    def project(p):
        act_ref[p * piece:(p + 1) * piece, :] = _dot(u_ref[p * piece:(p + 1) * piece, :], xt_ref[...])

    def weigh(p):
        for ii in (2 * p, 2 * p + 1):
            rows = slice(ii * PEER_NKEYS, (ii + 1) * PEER_NKEYS)
            w = _expert_weights(r2_ref, b2_ref, n1_ref, a1_ref, ii, tm)
            pt_ref[rows, :] = w * _gelu(act_ref[rows, :]).astype(BF16)

    project(0)
    for p in range(1, rows_per_chunk // 2):
        project(p)
        weigh(p - 1)
    weigh(rows_per_chunk // 2 - 1)
    acc_ref[...] += _dot(vt_ref[...], pt_ref[...])

    @pl.when(c == pl.num_programs(1) - 1)
    def _():
        o_ref[...] = _layernorm(DN_ALPHA * x_ref[...] + acc_ref[...].T, g_ref[...], b_ref[...])


def _peer_ffn(x1, r2, b2, n1, a1, u, vt, g, b, tm, ec=1024):
    t = x1.shape[0]
    rows_per_chunk = ec // PEER_NKEYS
    tok = pl.BlockSpec((tm, D_MODEL), lambda i, c: (i, 0))
    col = pl.BlockSpec((PEER_HEADS, PEER_NKEYS // BF16_ROWS, BF16_ROWS, tm), lambda i, c: (0, 0, 0, i))
    chunk_rows = pl.BlockSpec((PEER_HEADS, rows_per_chunk, tm), lambda i, c: (0, c, i))
    row = _resident((1, D_MODEL))
    return pl.pallas_call(
        functools.partial(_peer_ffn_kernel, rows_per_chunk=rows_per_chunk),
        grid=(t // tm, PEER_EXPERTS // ec),
        in_specs=[tok, col, col, chunk_rows, chunk_rows, pl.BlockSpec((ec, D_MODEL), lambda i, c: (c, 0)),
                  pl.BlockSpec((D_MODEL, ec), lambda i, c: (0, c)), row, row],
        out_specs=tok,
        out_shape=jax.ShapeDtypeStruct((t, D_MODEL), F32),
        scratch_shapes=[pltpu.VMEM((D_MODEL, tm), BF16), pltpu.VMEM((ec, tm), F32), pltpu.VMEM((ec, tm), BF16),
                        pltpu.VMEM((D_MODEL, tm), F32)],
        compiler_params=_params("arbitrary", "arbitrary"),
        name="peer_ffn",
    )(x1, r2, b2, n1, a1, u, vt, g, b)


def _cast_t_kernel(v_ref, o_ref):
    o_ref[...] = v_ref[...].T.astype(BF16)


def _cast_transpose(v, tr=512):
    rows, cols = v.shape
    return pl.pallas_call(
        _cast_t_kernel,
        grid=(rows // tr,),
        in_specs=[pl.BlockSpec((tr, cols), lambda i: (i, 0))],
        out_specs=pl.BlockSpec((cols, tr), lambda i: (0, i)),
        out_shape=jax.ShapeDtypeStruct((cols, rows), BF16),
        compiler_params=_params("arbitrary"),
        name="cast_transpose",
    )(v)


def _channel_sublayers(x, yc, yr, ya, wts, tm):
    x1 = _merge(x, yc, yr, ya, wts["gate"], wts["br_conv"], wts["br_rnn"], wts["br_attn"], wts["o"],
                wts["ln1_g"], wts["ln1_b"], tm)
    r2, b2, n1, a1 = _peer_sel(x1, wts["peer_wq"], wts["peer_keys"], min(tm, 256))
    return _peer_ffn(x1, r2, b2, n1, a1, wts["peer_u"], wts["peer_vt"], wts["ln2_g"], wts["ln2_b"], tm)


def kernel(x_prompt, x_sample, mem_prompt, cache_mem_k, cache_mem_v, state_conv_z, state_rglru_conv, state_rglru_h, w_in, conv_w, rg_conv_w, rg_conv_b, rg_wa, rg_ba, rg_wx, rg_bx, rg_lambda, w_mk, w_mv, w_br_conv, w_br_rnn, w_br_attn, w_o, ln1_g, ln1_b, peer_wq, peer_keys, peer_u, peer_v, ln2_g, ln2_b):
    assert w_in.shape[0] == DEPTH == 1
    bp, sp, _ = x_prompt.shape
    bs = x_sample.shape[0]
    l = 0
    c3 = 3 * CONV_WIDTH
    r2w = 2 * RNN_WIDTH
    win = w_in[l].astype(BF16)
    w_conv, w_rnn = win[:, :c3], win[:, c3:c3 + r2w]
    w_q = win[:, c3 + r2w:c3 + r2w + ATTN_WIDTH]
    row = lambda a: a[l].reshape(1, -1)
    wg = jnp.concatenate([rg_wa[l], rg_wx[l]], axis=-1).astype(BF16)
    wts = {
        "gate": win[:, c3 + r2w + ATTN_WIDTH:],
        "br_conv": w_br_conv[l].astype(BF16), "br_rnn": w_br_rnn[l].astype(BF16),
        "br_attn": w_br_attn[l].astype(BF16), "o": w_o[l].astype(BF16),
        "ln1_g": row(ln1_g), "ln1_b": row(ln1_b), "ln2_g": row(ln2_g), "ln2_b": row(ln2_b),
        "peer_wq": peer_wq[l].astype(BF16),
        "peer_keys": peer_keys[l].reshape(PEER_HEADS * 2, PEER_NKEYS, PEER_HALF).astype(BF16),
        "peer_u": peer_u[l].astype(BF16), "peer_vt": _cast_transpose(peer_v[l]),
    }
    rnn_args = (rg_conv_w[l], row(rg_conv_b), wg, row(rg_ba), row(rg_bx), row(rg_lambda))

    mk, mv = _mem_kv(mem_prompt.reshape(bp * N_MEM, D_MODEL), w_mk[l].astype(BF16), w_mv[l].astype(BF16))
    yc_p, cz_p = _conv_branch(x_prompt, w_conv, conv_w[l])
    yr_p, rc_p, h_p = _rnn_branch(x_prompt, w_rnn, *rnn_args)
    ya_p = _attn_branch(x_prompt, w_q, mk.reshape(bp, N_MEM, ATTN_WIDTH), mv.reshape(bp, N_MEM, ATTN_WIDTH))
    tp = bp * sp
    y_p = _channel_sublayers(x_prompt.reshape(tp, D_MODEL), yc_p.reshape(tp, CONV_WIDTH),
                             yr_p.reshape(tp, RNN_WIDTH), ya_p.reshape(tp, ATTN_WIDTH), wts, 512)

    xs = x_sample.reshape(bs, D_MODEL)
    z_s = _proj(xs, win[:, :c3 + r2w + ATTN_WIDTH])
    yc_s, yr_s, cz_s, rc_s, h_s = _sample_seq(
        z_s[:, :c3 + r2w], state_conv_z[l], state_rglru_conv[l], state_rglru_h[l], conv_w[l], *rnn_args)
    ya_s = _sample_attn(z_s[:, c3 + r2w:].reshape(bs, ATTN_HEADS, ATTN_HEAD_DIM), cache_mem_k, cache_mem_v)
    ya_s = ya_s.reshape(bs, ATTN_WIDTH)
    y_s = _channel_sublayers(xs, yc_s, yr_s, ya_s, wts, bs)

    hd = (ATTN_HEADS, ATTN_HEAD_DIM)
    return (y_p.reshape(bp, sp, D_MODEL), y_s.reshape(bs, 1, D_MODEL),
            mk.reshape(1, bp, N_MEM, *hd), mv.reshape(1, bp, N_MEM, *hd),
            cz_p[None], rc_p[None], h_p.reshape(1, bp, RNN_WIDTH),
            cz_s[None], rc_s[None], h_s[None])
```

```python
import functools

import jax
import jax.numpy as jnp
from jax import lax
from jax.experimental import pallas as pl
from jax.experimental.pallas import tpu as pltpu

F32 = jnp.float32
BF16 = jnp.bfloat16

D_MODEL = 2048
N_MEM = 256
CONV_WIDTH = 1024
CONV_K = 3
RNN_WIDTH = 1024
RNN_HEADS = 8
RNN_HEAD_DIM = RNN_WIDTH // RNN_HEADS
RNN_CONV_K = 4
RG_C = 8.0
ATTN_HEADS = 4
ATTN_HEAD_DIM = 256
ATTN_WIDTH = ATTN_HEADS * ATTN_HEAD_DIM
N_BRANCH = 3
PEER_HEADS = 8
PEER_NKEYS = 128
PEER_EXPERTS = PEER_NKEYS * PEER_NKEYS
PEER_TOPK = 16
PEER_HALF = 128
PEER_CHUNK = 1024
MIX_WIDTH = 3 * CONV_WIDTH + 2 * RNN_WIDTH + ATTN_WIDTH
RNN_X_BLOCK = 3 * CONV_WIDTH // RNN_WIDTH
RNN_GATE_BLOCK = RNN_X_BLOCK + 1
ATTN_Q_BLOCK = (3 * CONV_WIDTH + 2 * RNN_WIDTH) // ATTN_WIDTH
DEPTH = 1
DN_ALPHA = (2.0 * DEPTH) ** 0.25
LN_EPS = 1e-5

VMEM_LIMIT_V7X = 56 * 1024 * 1024
SUBLANES = 8
LANES = 128
BF16_ROWS = 2 * SUBLANES
NOT_SELECTED_RANK = 99.0


def _params(*sem):
    return pltpu.CompilerParams(dimension_semantics=sem, vmem_limit_bytes=VMEM_LIMIT_V7X)


def _resident(shape, index=None):
    index = (0,) * len(shape) if index is None else tuple(index)
    return pl.BlockSpec(shape, lambda *_: index, pipeline_mode=pl.Buffered(1))


def _dot(a, b):
    return jnp.dot(a, b, preferred_element_type=F32)


def _dot_nt(a, b):
    return lax.dot_general(a, b, (((1,), (1,)), ((), ())), preferred_element_type=F32)


def _sigmoid(x):
    return 1.0 / (1.0 + jnp.exp(-x))


def _gelu(x):
    return 0.5 * x * (1.0 + jnp.tanh(0.7978845608028654 * (x + 0.044715 * (x * x * x))))


def _softplus(x):
    return jnp.maximum(x, 0.0) + jnp.log1p(jnp.exp(-jnp.abs(x)))


def _layernorm(y, g, b):
    mu = jnp.mean(y, axis=-1, keepdims=True)
    yc = y - mu
    var = jnp.mean(yc * yc, axis=-1, keepdims=True)
    return yc * lax.rsqrt(var + LN_EPS) * g + b


def _mem_kv_kernel(m_ref, wk_ref, wv_ref, k_ref, v_ref):
    mb = m_ref[...].astype(BF16)
    k_ref[...] = _dot(mb, wk_ref[...])
    v_ref[...] = _dot(mb, wv_ref[...])


def _mem_kv(mem, wk, wv, tm=256):
    m = mem.shape[0]
    return pl.pallas_call(
        _mem_kv_kernel,
        grid=(m // tm,),
        in_specs=[pl.BlockSpec((tm, D_MODEL), lambda i: (i, 0)),
                  _resident((D_MODEL, ATTN_WIDTH)), _resident((D_MODEL, ATTN_WIDTH))],
        out_specs=[pl.BlockSpec((tm, ATTN_WIDTH), lambda i: (i, 0))] * 2,
        out_shape=[jax.ShapeDtypeStruct((m, ATTN_WIDTH), F32)] * 2,
        compiler_params=_params("arbitrary"),
        name="mem_kv",
    )(mem, wk, wv)


def _conv_kernel(x_ref, w_ref, cw_ref, y_ref, st_ref, pb_ref, *, ts):
    @pl.when(pl.program_id(1) == 0)
    def _():
        pb_ref[0:SUBLANES, :] = jnp.zeros((SUBLANES, CONV_WIDTH), F32)

    z = _dot(x_ref[0].astype(BF16), w_ref[...])
    p = z[:, CONV_WIDTH:2 * CONV_WIDTH] * z[:, 2 * CONV_WIDTH:]
    pb_ref[SUBLANES:SUBLANES + ts, :] = p
    p1 = pb_ref[SUBLANES - 1:SUBLANES - 1 + ts, :]
    p2 = pb_ref[SUBLANES - 2:SUBLANES - 2 + ts, :]
    cw = cw_ref[...]
    y = z[:, :CONV_WIDTH] * (cw[0:1] * p2 + cw[1:2] * p1 + cw[2:3] * p)
    y_ref[0] = y.astype(BF16)
    tail = pb_ref[ts:ts + SUBLANES, :]
    st_ref[0] = tail[SUBLANES - (CONV_K - 1):]
    pb_ref[0:SUBLANES, :] = tail


def _conv_branch(x, win, cw, ts=512):
    b, s, _ = x.shape
    return pl.pallas_call(
        functools.partial(_conv_kernel, ts=ts),
        grid=(b, s // ts),
        in_specs=[pl.BlockSpec((1, ts, D_MODEL), lambda i, j: (i, j, 0)),
                  _resident((D_MODEL, 3 * CONV_WIDTH)), _resident((CONV_K, CONV_WIDTH))],
        out_specs=[pl.BlockSpec((1, ts, CONV_WIDTH), lambda i, j: (i, j, 0)),
                   pl.BlockSpec((1, CONV_K - 1, CONV_WIDTH), lambda i, j: (i, 0, 0))],
        out_shape=[jax.ShapeDtypeStruct((b, s, CONV_WIDTH), BF16),
                   jax.ShapeDtypeStruct((b, CONV_K - 1, CONV_WIDTH), F32)],
        scratch_shapes=[pltpu.VMEM((SUBLANES + ts, CONV_WIDTH), F32)],
        compiler_params=_params("arbitrary", "arbitrary"),
        name="conv_branch",
    )(x, win, cw)


def _rglru_gates(xr, wg_ref, ba, bx, lam):
    xb = xr.astype(BF16)
    gr, gi = [], []
    for h in range(RNN_HEADS):
        g = _dot(xb[:, h * RNN_HEAD_DIM:(h + 1) * RNN_HEAD_DIM], wg_ref[h])
        gr.append(g[:, :RNN_HEAD_DIM])
        gi.append(g[:, RNN_HEAD_DIM:])
    r = _sigmoid(jnp.concatenate(gr, axis=1) + ba)
    i = _sigmoid(jnp.concatenate(gi, axis=1) + bx)
    log_a = -RG_C * r * _softplus(-lam)
    a = jnp.exp(log_a)
    th = jnp.tanh(log_a)
    w = -2.0 * th / (1.0 - th)
    u = jnp.where(w > 0.0, w * lax.rsqrt(w), 0.0) * (i * xr)
    return a, u


def _scan_rows(a, u, h_in, ts):
    pos = lax.broadcasted_iota(jnp.int32, a.shape, 0) % SUBLANES
    d = 1
    while d < SUBLANES:
        keep = pos >= d
        u = jnp.where(keep, u + a * pltpu.roll(u, d, axis=0), u)
        a = jnp.where(keep, a * pltpu.roll(a, d, axis=0), a)
        d *= 2
    groups = []
    h = h_in
    for g in range(ts // SUBLANES):
        rows = slice(g * SUBLANES, (g + 1) * SUBLANES)
        hg = a[rows] * h + u[rows]
        groups.append(hg)
        h = hg[SUBLANES - 1:SUBLANES]
    return jnp.concatenate(groups, axis=0)


def _rnn_kernel(x_ref, wx_ref, wgate_ref, cw_ref, cb_ref, wg_ref, ba_ref, bx_ref, lam_ref,
                y_ref, buf_ref, h_ref, rb_ref, hc_ref, *, ts):
    @pl.when(pl.program_id(1) == 0)
    def _():
        rb_ref[0:SUBLANES, :] = jnp.zeros((SUBLANES, RNN_WIDTH), F32)
        hc_ref[...] = jnp.zeros((1, RNN_WIDTH), F32)

    xb = x_ref[0].astype(BF16)
    rx = _dot(xb, wx_ref[...])
    rgate = _dot(xb, wgate_ref[...])
    rb_ref[SUBLANES:SUBLANES + ts, :] = rx
    cw = cw_ref[...]
    xr = cw[3:4] * rx + cb_ref[...]
    for k in range(RNN_CONV_K - 1):
        off = SUBLANES - (RNN_CONV_K - 1) + k
        xr = xr + cw[k:k + 1] * rb_ref[off:off + ts, :]
    a, u = _rglru_gates(xr, wg_ref, ba_ref[...], bx_ref[...], lam_ref[...])
    h = _scan_rows(a, u, hc_ref[...], ts)
    y_ref[0] = (h * _gelu(rgate)).astype(BF16)
    hc_ref[...] = h[ts - 1:ts, :]
    h_ref[0] = h[ts - 1:ts, :]
    tail = rb_ref[ts:ts + SUBLANES, :]
    buf_ref[0] = tail[SUBLANES - (RNN_CONV_K - 1):]
    rb_ref[0:SUBLANES, :] = tail


def _rnn_branch(x, win, cw, cb, wg, ba, bx, lam, ts=256):
    b, s, _ = x.shape
    row = _resident((1, RNN_WIDTH))
    return pl.pallas_call(
        functools.partial(_rnn_kernel, ts=ts),
        grid=(b, s // ts),
        in_specs=[pl.BlockSpec((1, ts, D_MODEL), lambda i, j: (i, j, 0)),
                  _resident((D_MODEL, RNN_WIDTH), (0, RNN_X_BLOCK)),
                  _resident((D_MODEL, RNN_WIDTH), (0, RNN_GATE_BLOCK)), _resident((RNN_CONV_K, RNN_WIDTH)), row,
                  _resident((RNN_HEADS, RNN_HEAD_DIM, 2 * RNN_HEAD_DIM)), row, row, row],
        out_specs=[pl.BlockSpec((1, ts, RNN_WIDTH), lambda i, j: (i, j, 0)),
                   pl.BlockSpec((1, RNN_CONV_K - 1, RNN_WIDTH), lambda i, j: (i, 0, 0)),
                   pl.BlockSpec((1, 1, RNN_WIDTH), lambda i, j: (i, 0, 0))],
        out_shape=[jax.ShapeDtypeStruct((b, s, RNN_WIDTH), BF16),
                   jax.ShapeDtypeStruct((b, RNN_CONV_K - 1, RNN_WIDTH), F32),
                   jax.ShapeDtypeStruct((b, 1, RNN_WIDTH), F32)],
        scratch_shapes=[pltpu.VMEM((SUBLANES + ts, RNN_WIDTH), F32), pltpu.VMEM((1, RNN_WIDTH), F32)],
        compiler_params=_params("arbitrary", "arbitrary"),
        name="rnn_branch",
    )(x, win, win, cw, cb, wg, ba, bx, lam)


def _attn_kernel(x_ref, w_ref, k_ref, v_ref, y_ref):
    q = _dot(x_ref[0].astype(BF16), w_ref[...]).astype(BF16)
    kb = k_ref[0].astype(BF16)
    vb = v_ref[0].astype(BF16)
    outs = []
    for h in range(ATTN_HEADS):
        sl = slice(h * ATTN_HEAD_DIM, (h + 1) * ATTN_HEAD_DIM)
        sc = _dot_nt(q[:, sl], kb[:, sl]) * (ATTN_HEAD_DIM ** -0.5)
        e = jnp.exp(sc - jnp.max(sc, axis=-1, keepdims=True))
        p = e / jnp.sum(e, axis=-1, keepdims=True)
        outs.append(_dot(p.astype(BF16), vb[:, sl]))
    y_ref[0] = jnp.concatenate(outs, axis=1).astype(BF16)


def _attn_branch(x, win, mk, mv, ts=512):
    b, s, _ = x.shape
    return pl.pallas_call(
        _attn_kernel,
        grid=(b, s // ts),
        in_specs=[pl.BlockSpec((1, ts, D_MODEL), lambda i, j: (i, j, 0)),
                  _resident((D_MODEL, ATTN_WIDTH), (0, ATTN_Q_BLOCK)),
                  pl.BlockSpec((1, N_MEM, ATTN_WIDTH), lambda i, j: (i, 0, 0)),
                  pl.BlockSpec((1, N_MEM, ATTN_WIDTH), lambda i, j: (i, 0, 0))],
        out_specs=pl.BlockSpec((1, ts, ATTN_WIDTH), lambda i, j: (i, j, 0)),
        out_shape=jax.ShapeDtypeStruct((b, s, ATTN_WIDTH), BF16),
        compiler_params=_params("arbitrary", "arbitrary"),
        name="attn_branch",
    )(x, win, mk, mv)


def _sample_seq_kernel(z_ref, cst_ref, rst_ref, h0_ref, ccw_ref, rcw_ref, rcb_ref, wg_ref,
                       ba_ref, bx_ref, lam_ref, yc_ref, yr_ref, cz_ref, rc_ref, h_ref):
    z = z_ref[...]
    p = z[:, CONV_WIDTH:2 * CONV_WIDTH] * z[:, 2 * CONV_WIDTH:3 * CONV_WIDTH]
    cst = cst_ref[...]
    ccw = ccw_ref[...]
    conv_y = ccw[0:1] * cst[:, 0, :] + ccw[1:2] * cst[:, 1, :] + ccw[2:3] * p
    yc_ref[...] = (z[:, :CONV_WIDTH] * conv_y).astype(BF16)
    cz_ref[:, 0, :] = cst[:, 1, :]
    cz_ref[:, 1, :] = p
    rx = z[:, 3 * CONV_WIDTH:3 * CONV_WIDTH + RNN_WIDTH]
    rst = rst_ref[...]
    rcw = rcw_ref[...]
    xr = rcw[0:1] * rst[:, 0, :] + rcw[1:2] * rst[:, 1, :] + rcw[2:3] * rst[:, 2, :] + rcw[3:4] * rx + rcb_ref[...]
    a, u = _rglru_gates(xr, wg_ref, ba_ref[...], bx_ref[...], lam_ref[...])
    h = a * h0_ref[...] + u
    yr_ref[...] = (h * _gelu(z[:, 3 * CONV_WIDTH + RNN_WIDTH:])).astype(BF16)
    h_ref[...] = h
    rc_ref[:, 0, :] = rst[:, 1, :]
    rc_ref[:, 1, :] = rst[:, 2, :]
    rc_ref[:, 2, :] = rx


def _sample_seq(z, cst, rst, h0, ccw, rcw, rcb, wg, ba, bx, lam, tb=32):
    n = z.shape[0]
    row = _resident((1, RNN_WIDTH))

    def tok(*tail):
        nd = len(tail)
        return pl.BlockSpec((tb,) + tail, lambda i: (i,) + (0,) * nd)

    return pl.pallas_call(
        _sample_seq_kernel,
        grid=(n // tb,),
        in_specs=[tok(3 * CONV_WIDTH + 2 * RNN_WIDTH), tok(CONV_K - 1, CONV_WIDTH),
                  tok(RNN_CONV_K - 1, RNN_WIDTH), tok(RNN_WIDTH),
                  _resident((CONV_K, CONV_WIDTH)), _resident((RNN_CONV_K, RNN_WIDTH)), row,
                  _resident((RNN_HEADS, RNN_HEAD_DIM, 2 * RNN_HEAD_DIM)), row, row, row],
        out_specs=[tok(CONV_WIDTH), tok(RNN_WIDTH),
                   tok(CONV_K - 1, CONV_WIDTH), tok(RNN_CONV_K - 1, RNN_WIDTH), tok(RNN_WIDTH)],
        out_shape=[jax.ShapeDtypeStruct((n, CONV_WIDTH), BF16), jax.ShapeDtypeStruct((n, RNN_WIDTH), BF16),
                   jax.ShapeDtypeStruct((n, CONV_K - 1, CONV_WIDTH), F32),
                   jax.ShapeDtypeStruct((n, RNN_CONV_K - 1, RNN_WIDTH), F32),
                   jax.ShapeDtypeStruct((n, RNN_WIDTH), F32)],
        compiler_params=_params("arbitrary"),
        name="sample_seq",
    )(z, cst, rst, h0, ccw, rcw, rcb, wg, ba, bx, lam)


def _sample_attn_kernel(q_ref, k_ref, v_ref, y_ref):
    q = q_ref[...]
    sc = jnp.sum(k_ref[0] * q[:, None], axis=-1, keepdims=True) * (ATTN_HEAD_DIM ** -0.5)
    e = jnp.exp(sc - jnp.max(sc, axis=1, keepdims=True))
    pr = e / jnp.sum(e, axis=1, keepdims=True)
    y_ref[...] = jnp.sum(pr * v_ref[0], axis=1).astype(BF16)


def _sample_attn(q, ck, cv, tb=2):
    n = q.shape[0]
    tok = pl.BlockSpec((tb, ATTN_HEADS, ATTN_HEAD_DIM), lambda i: (i, 0, 0))
    mem = pl.BlockSpec((1, tb, N_MEM, ATTN_HEADS, ATTN_HEAD_DIM), lambda i: (0, i, 0, 0, 0))
    return pl.pallas_call(
        _sample_attn_kernel,
        grid=(n // tb,),
        in_specs=[tok, mem, mem],
        out_specs=tok,
        out_shape=jax.ShapeDtypeStruct((n, ATTN_HEADS, ATTN_HEAD_DIM), BF16),
        compiler_params=_params("arbitrary"),
        name="sample_attn",
    )(q, ck, cv)


def _proj_kernel(x_ref, w_ref, o_ref):
    o_ref[...] = _dot(x_ref[...].astype(BF16), w_ref[...])


def _proj(x, w, n, tn=1024):
    m, k = x.shape
    return pl.pallas_call(
        _proj_kernel,
        grid=(n // tn,),
        in_specs=[_resident((m, k)), pl.BlockSpec((k, tn), lambda j: (0, j))],
        out_specs=pl.BlockSpec((m, tn), lambda j: (0, j)),
        out_shape=jax.ShapeDtypeStruct((m, n), F32),
        compiler_params=_params("arbitrary"),
        name="sample_proj",
    )(x, w)


def _merge_kernel(x_ref, yc_ref, yr_ref, ya_ref, wg0_ref, wg1_ref, wg2_ref, wc_ref, wr_ref, wa_ref, wo_ref,
                  g_ref, b_ref, o_ref, xb_ref):
    n = pl.program_id(1)

    @pl.when(n == 0)
    def _():
        xb_ref[...] = x_ref[...].astype(BF16)
        o_ref[...] = jnp.zeros_like(o_ref)

    xb = xb_ref[...]
    merged = (_sigmoid(_dot(xb, wg0_ref[...])) * _dot(yc_ref[...], wc_ref[...])
              + _sigmoid(_dot(xb, wg1_ref[...])) * _dot(yr_ref[...], wr_ref[...])
              + _sigmoid(_dot(xb, wg2_ref[...])) * _dot(ya_ref[...], wa_ref[...]))
    o_ref[...] += _dot(merged.astype(BF16), wo_ref[...])

    @pl.when(n == pl.num_programs(1) - 1)
    def _():
        o_ref[...] = _layernorm(DN_ALPHA * x_ref[...] + o_ref[...], g_ref[...], b_ref[...])


def _merge(x, yc, yr, ya, win, wc, wr, wa, wo, g, b, tm, tn=512):
    t = x.shape[0]
    nb = D_MODEL // tn
    gate0 = (win.shape[1] - N_BRANCH * D_MODEL) // tn
    tok_f = pl.BlockSpec((tm, D_MODEL), lambda i, n: (i, 0))
    tok_x = pl.BlockSpec((tm, D_MODEL), lambda i, n: (i, 0), pipeline_mode=pl.Buffered(1))
    tok_h = pl.BlockSpec((tm, CONV_WIDTH), lambda i, n: (i, 0))
    gate = [pl.BlockSpec((D_MODEL, tn), functools.partial(lambda i, n, k: (0, gate0 + k * nb + n), k=k))
            for k in range(N_BRANCH)]
    br = pl.BlockSpec((CONV_WIDTH, tn), lambda i, n: (0, n))
    row = _resident((1, D_MODEL))
    return pl.pallas_call(
        _merge_kernel,
        grid=(t // tm, nb),
        in_specs=[tok_x, tok_h, tok_h, tok_h] + gate + [br, br, br,
                  pl.BlockSpec((tn, D_MODEL), lambda i, n: (n, 0)), row, row],
        out_specs=tok_f,
        out_shape=jax.ShapeDtypeStruct((t, D_MODEL), F32),
        scratch_shapes=[pltpu.VMEM((tm, D_MODEL), BF16)],
        compiler_params=_params("arbitrary", "arbitrary"),
        name="merge",
    )(x, yc, yr, ya, win, win, win, wc, wr, wa, wo, g, b)


def _top_rows(sc, k):
    n = sc.shape[0]
    rows = lax.broadcasted_iota(jnp.int32, sc.shape, 0)
    rank = jnp.full(sc.shape, NOT_SELECTED_RANK, F32)
    rem = sc
    vals = []
    for r in range(k):
        m = jnp.max(rem, axis=0, keepdims=True)
        first = jnp.min(jnp.where(rem == m, rows, n), axis=0, keepdims=True)
        hit = rows == first
        rank = jnp.where(hit, float(r), rank)
        rem = jnp.where(hit, -jnp.inf, rem)
        vals.append(m)
    return jnp.concatenate(vals, axis=0), rank


_WIDE_R1 = 8


def _candidates(v1, v2):
    blocks = [v1[0:1] + v2]
    blocks += [v1[r:r + 1] + v2[:SUBLANES] for r in range(1, _WIDE_R1)]
    blocks.append(v1[_WIDE_R1:] + v2[0:1])
    return jnp.concatenate(blocks, axis=0)


def _row_counts(chosen):
    cnt = jnp.where(chosen, 1.0, 0.0)
    counts = [jnp.sum(cnt[0:PEER_TOPK], axis=0, keepdims=True)]
    for r in range(1, _WIDE_R1):
        lo = PEER_TOPK + (r - 1) * SUBLANES
        counts.append(jnp.sum(cnt[lo:lo + SUBLANES], axis=0, keepdims=True))
    lo = PEER_TOPK + (_WIDE_R1 - 1) * SUBLANES
    counts += [cnt[lo + r:lo + r + 1] for r in range(PEER_TOPK - _WIDE_R1)]
    return counts


def _select_exact(s1, s2):
    v1, rank1 = _top_rows(s1, PEER_TOPK)
    v2, rank2 = _top_rows(s2, PEER_TOPK)
    cand = _candidates(v1, v2)
    _, crank = _top_rows(cand, PEER_TOPK)
    chosen = crank < NOT_SELECTED_RANK
    z = jnp.sum(jnp.where(chosen, jnp.exp(cand - cand[0:1]), 0.0), axis=0, keepdims=True)
    counts = _row_counts(chosen)
    n1 = jnp.zeros(s1.shape, F32)
    for r in range(PEER_TOPK):
        n1 = jnp.where(rank1 == float(r), counts[r], n1)
    return rank2, jnp.exp(s2 - v2[0:1]) / z, n1, jnp.exp(s1 - v1[0:1])


def _sorting_network(lo, hi):
    def merge(lo, hi, r):
        step = r * 2
        if step < hi - lo:
            yield from merge(lo, hi, step)
            yield from merge(lo + r, hi, step)
            yield from ((i, i + r) for i in range(lo + r, hi - r, step))
        else:
            yield (lo, lo + r)

    if hi - lo >= 1:
        mid = lo + (hi - lo) // 2
        yield from _sorting_network(lo, mid)
        yield from _sorting_network(mid + 1, hi)
        yield from merge(lo, hi, 1)


def _top_values(sc, k):
    assert sc.shape[0] == SUBLANES * k
    x = [sc[SUBLANES * v:SUBLANES * (v + 1)] for v in range(k)]
    for a, b in _sorting_network(0, k - 1):
        x[a], x[b] = jnp.maximum(x[a], x[b]), jnp.minimum(x[a], x[b])
    vals = []
    for r in range(k):
        m = jnp.max(x[0], axis=0, keepdims=True)
        vals.append(m)
        live = k - r
        if live > 1:
            hit = x[0] == m
            for p in range(live - 1):
                x[p] = jnp.where(hit, x[p + 1], x[p])
    return vals


def _count_at_least(sc, v):
    return jnp.sum(jnp.where(sc >= v, 1.0, 0.0), axis=0, keepdims=True)


def _select_distinct(s1, s2):
    v1 = _top_values(s1, PEER_TOPK)
    v2 = _top_values(s2, PEER_TOPK)
    cand = _candidates(jnp.concatenate(v1, axis=0), jnp.concatenate(v2, axis=0))
    rem = cand
    for r in range(PEER_TOPK):
        last = jnp.max(rem, axis=0, keepdims=True)
        rem = jnp.where(rem == last, -jnp.inf, rem)
    chosen = rem == -jnp.inf
    most = jnp.maximum(jnp.maximum(_count_at_least(s1, v1[-1]), _count_at_least(s2, v2[-1])),
                       _count_at_least(cand, last))
    z = jnp.sum(jnp.where(chosen, jnp.exp(cand - cand[0:1]), 0.0), axis=0, keepdims=True)
    counts = _row_counts(chosen)
    n1 = jnp.zeros(s1.shape, F32)
    rank2 = jnp.full(s2.shape, NOT_SELECTED_RANK, F32)
    for r in range(PEER_TOPK):
        n1 = jnp.where(s1 == v1[r], counts[r], n1)
        rank2 = jnp.where(s2 == v2[r], float(r), rank2)
    return (rank2, jnp.exp(s2 - v2[0]) / z, n1, jnp.exp(s1 - v1[0])), most


def _peer_sel_kernel(x_ref, wq_ref, keys_ref, r2_ref, b2_ref, n1_ref, a1_ref, s_ref):
    q = _dot(x_ref[...].astype(BF16), wq_ref[...]).astype(BF16)
    ts = q.shape[0]

    def store(h, cols, sel):
        rank2, b2, n1, a1 = sel
        r2_ref[h, :, cols] = rank2.astype(BF16)
        b2_ref[h, :, cols] = b2.astype(BF16)
        n1_ref[h, :, cols] = n1
        a1_ref[h, :, cols] = a1

    for h in range(PEER_HEADS):
        s_ref[0] = _dot_nt(keys_ref[2 * h], q[:, (2 * h) * PEER_HALF:(2 * h + 1) * PEER_HALF])
        s_ref[1] = _dot_nt(keys_ref[2 * h + 1], q[:, (2 * h + 1) * PEER_HALF:(2 * h + 2) * PEER_HALF])
        most = None
        for c in range(ts // LANES):
            cols = slice(c * LANES, (c + 1) * LANES)
            sel, m = _select_distinct(s_ref[0, :, cols], s_ref[1, :, cols])
            store(h, cols, sel)
            most = m if most is None else jnp.maximum(most, m)

        @pl.when(jnp.max(most) > float(PEER_TOPK))
        def _():
            store(h, slice(None), _select_exact(s_ref[0], s_ref[1]))


def _peer_sel(x1, wq, keys, ts):
    t = x1.shape[0]
    sel = pl.BlockSpec((PEER_HEADS, PEER_NKEYS, ts), lambda i: (0, 0, i))
    shp = lambda dt: jax.ShapeDtypeStruct((PEER_HEADS, PEER_NKEYS, t), dt)
    r2, b2, n1, a1 = pl.pallas_call(
        _peer_sel_kernel,
        grid=(t // ts,),
        in_specs=[pl.BlockSpec((ts, D_MODEL), lambda i: (i, 0)),
                  _resident((D_MODEL, PEER_HEADS * 2 * PEER_HALF)),
                  _resident((PEER_HEADS * 2, PEER_NKEYS, PEER_HALF))],
        out_specs=[sel] * 4,
        out_shape=[shp(BF16), shp(BF16), shp(F32), shp(F32)],
        scratch_shapes=[pltpu.VMEM((2, PEER_NKEYS, ts), F32)],
        compiler_params=_params("arbitrary"),
        name="peer_sel",
    )(x1, wq, keys)
    grouped = (PEER_HEADS, PEER_NKEYS // BF16_ROWS, BF16_ROWS, t)
    return r2.reshape(grouped), b2.reshape(grouped), n1, a1


def _expert_weights(r2_ref, b2_ref, n1_ref, a1_ref, ii, tm):
    w = None
    for h in range(PEER_HEADS):
        nrow = jnp.broadcast_to(n1_ref[h, ii:ii + 1, :], (BF16_ROWS, tm)).astype(BF16)
        arow = jnp.broadcast_to(a1_ref[h, ii:ii + 1, :], (BF16_ROWS, tm)).astype(BF16)
        wh = jnp.where(r2_ref[h] < nrow[None], b2_ref[h], jnp.zeros((), BF16)) * arow[None]
        w = wh if w is None else w + wh
    return w.reshape(PEER_NKEYS, tm)


def _peer_ffn_kernel(x_ref, r2_ref, b2_ref, n1_ref, a1_ref, u_ref, vt_ref, g_ref, b_ref, o_ref,
                     xt_ref, act_ref, pt_ref, acc_ref, *, rows_per_chunk):
    c = pl.program_id(1)
    tm = x_ref.shape[0]
    piece = 2 * PEER_NKEYS

    @pl.when(c == 0)
    def _():
        xt_ref[...] = x_ref[...].T.astype(BF16)
        acc_ref[...] = jnp.zeros_like(acc_ref)

    def project(p):
        act_ref[p * piece:(p + 1) * piece, :] = _dot(u_ref[p * piece:(p + 1) * piece, :], xt_ref[...])

    def weigh(p):
        for ii in (2 * p, 2 * p + 1):
            rows = slice(ii * PEER_NKEYS, (ii + 1) * PEER_NKEYS)
            w = _expert_weights(r2_ref, b2_ref, n1_ref, a1_ref, ii, tm)
            pt_ref[rows, :] = w * _gelu(act_ref[rows, :]).astype(BF16)

    project(0)
    for p in range(1, rows_per_chunk // 2):
        project(p)
        weigh(p - 1)
    weigh(rows_per_chunk // 2 - 1)
    acc_ref[...] += _dot(vt_ref[...], pt_ref[...])

    @pl.when(c == pl.num_programs(1) - 1)
    def _():
        o_ref[...] = _layernorm(DN_ALPHA * x_ref[...] + acc_ref[...].T, g_ref[...], b_ref[...])


def _peer_ffn(x1, r2, b2, n1, a1, u, vt, g, b, tm):
    t = x1.shape[0]
    ec = PEER_CHUNK
    rows_per_chunk = ec // PEER_NKEYS
    tok = pl.BlockSpec((tm, D_MODEL), lambda i, c: (i, 0))
    col = pl.BlockSpec((PEER_HEADS, PEER_NKEYS // BF16_ROWS, BF16_ROWS, tm), lambda i, c: (0, 0, 0, i))
    chunk_rows = pl.BlockSpec((PEER_HEADS, rows_per_chunk, tm), lambda i, c: (0, c, i))
    row = _resident((1, D_MODEL))
    return pl.pallas_call(
        functools.partial(_peer_ffn_kernel, rows_per_chunk=rows_per_chunk),
        grid=(t // tm, PEER_EXPERTS // ec),
        in_specs=[tok, col, col, chunk_rows, chunk_rows,
                  pl.BlockSpec((ec, D_MODEL), lambda i, c: (c, 0)),
                  pl.BlockSpec((D_MODEL, ec), lambda i, c: (0, c)), row, row],
        out_specs=tok,
        out_shape=jax.ShapeDtypeStruct((t, D_MODEL), F32),
        scratch_shapes=[pltpu.VMEM((D_MODEL, tm), BF16), pltpu.VMEM((ec, tm), F32), pltpu.VMEM((ec, tm), BF16),
                        pltpu.VMEM((D_MODEL, tm), F32)],
        compiler_params=_params("arbitrary", "arbitrary"),
        name="peer_ffn",
    )(x1, r2, b2, n1, a1, u, vt, g, b)


def _cast_t_kernel(v_ref, o_ref):
    o_ref[...] = v_ref[...].T.astype(BF16)


def _cast_transpose(v, tr=512):
    rows, cols = v.shape
    return pl.pallas_call(
        _cast_t_kernel,
        grid=(rows // tr,),
        in_specs=[pl.BlockSpec((tr, cols), lambda i: (i, 0))],
        out_specs=pl.BlockSpec((cols, tr), lambda i: (0, i)),
        out_shape=jax.ShapeDtypeStruct((cols, rows), BF16),
        compiler_params=_params("arbitrary"),
        name="cast_transpose",
    )(v)


def _channel_sublayers(x, yc, yr, ya, wts, tm):
    x1 = _merge(x, yc, yr, ya, wts["in"], wts["br_conv"], wts["br_rnn"], wts["br_attn"], wts["o"],
                wts["ln1_g"], wts["ln1_b"], tm)
    r2, b2, n1, a1 = _peer_sel(x1, wts["peer_wq"], wts["peer_keys"], min(tm, 256))
    return _peer_ffn(x1, r2, b2, n1, a1, wts["peer_u"], wts["peer_vt"], wts["ln2_g"], wts["ln2_b"], tm)


def kernel(x_prompt, x_sample, mem_prompt, cache_mem_k, cache_mem_v, state_conv_z, state_rglru_conv, state_rglru_h, w_in, conv_w, rg_conv_w, rg_conv_b, rg_wa, rg_ba, rg_wx, rg_bx, rg_lambda, w_mk, w_mv, w_br_conv, w_br_rnn, w_br_attn, w_o, ln1_g, ln1_b, peer_wq, peer_keys, peer_u, peer_v, ln2_g, ln2_b):
    assert w_in.shape[0] == DEPTH == 1
    bp, sp, _ = x_prompt.shape
    bs = x_sample.shape[0]
    l = 0
    seq_width = MIX_WIDTH - ATTN_WIDTH
    win = w_in[l].astype(BF16)
    row = lambda a: a[l].reshape(1, -1)
    wg = jnp.concatenate([rg_wa[l], rg_wx[l]], axis=-1).astype(BF16)
    wts = {
        "in": win,
        "br_conv": w_br_conv[l].astype(BF16), "br_rnn": w_br_rnn[l].astype(BF16),
        "br_attn": w_br_attn[l].astype(BF16), "o": w_o[l].astype(BF16),
        "ln1_g": row(ln1_g), "ln1_b": row(ln1_b), "ln2_g": row(ln2_g), "ln2_b": row(ln2_b),
        "peer_wq": peer_wq[l].astype(BF16),
        "peer_keys": peer_keys[l].reshape(PEER_HEADS * 2, PEER_NKEYS, PEER_HALF).astype(BF16),
        "peer_u": peer_u[l].astype(BF16), "peer_vt": _cast_transpose(peer_v[l]),
    }
    rnn_args = (rg_conv_w[l], row(rg_conv_b), wg, row(rg_ba), row(rg_bx), row(rg_lambda))

    mk, mv = _mem_kv(mem_prompt.reshape(bp * N_MEM, D_MODEL), w_mk[l].astype(BF16), w_mv[l].astype(BF16))
    yc_p, cz_p = _conv_branch(x_prompt, win, conv_w[l])
    yr_p, rc_p, h_p = _rnn_branch(x_prompt, win, *rnn_args)
    ya_p = _attn_branch(x_prompt, win, mk.reshape(bp, N_MEM, ATTN_WIDTH), mv.reshape(bp, N_MEM, ATTN_WIDTH))
    tp = bp * sp
    y_p = _channel_sublayers(x_prompt.reshape(tp, D_MODEL), yc_p.reshape(tp, CONV_WIDTH),
                             yr_p.reshape(tp, RNN_WIDTH), ya_p.reshape(tp, ATTN_WIDTH), wts, 512)

    xs = x_sample.reshape(bs, D_MODEL)
    z_s = _proj(xs, win, MIX_WIDTH)
    yc_s, yr_s, cz_s, rc_s, h_s = _sample_seq(
        z_s, state_conv_z[l], state_rglru_conv[l], state_rglru_h[l], conv_w[l], *rnn_args)
    ya_s = _sample_attn(z_s[:, seq_width:].reshape(bs, ATTN_HEADS, ATTN_HEAD_DIM), cache_mem_k, cache_mem_v)
    ya_s = ya_s.reshape(bs, ATTN_WIDTH)
    y_s = _channel_sublayers(xs, yc_s, yr_s, ya_s, wts, bs)

    hd = (ATTN_HEADS, ATTN_HEAD_DIM)
    return (y_p.reshape(bp, sp, D_MODEL), y_s.reshape(bs, 1, D_MODEL),
            mk.reshape(1, bp, N_MEM, *hd), mv.reshape(1, bp, N_MEM, *hd),
            cz_p[None], rc_p[None], h_p.reshape(1, bp, RNN_WIDTH),
            cz_s[None], rc_s[None], h_s[None])
```

```python
import functools

import jax
import jax.numpy as jnp
from jax import lax
from jax.experimental import pallas as pl
from jax.experimental.pallas import tpu as pltpu

F32 = jnp.float32
BF16 = jnp.bfloat16

D_MODEL = 2048
N_MEM = 256
CONV_WIDTH = 1024
CONV_K = 3
RNN_WIDTH = 1024
RNN_HEADS = 8
RNN_HEAD_DIM = RNN_WIDTH // RNN_HEADS
RNN_CONV_K = 4
RG_C = 8.0
ATTN_HEADS = 4
ATTN_HEAD_DIM = 256
ATTN_WIDTH = ATTN_HEADS * ATTN_HEAD_DIM
N_BRANCH = 3
PEER_HEADS = 8
PEER_NKEYS = 128
PEER_EXPERTS = PEER_NKEYS * PEER_NKEYS
PEER_TOPK = 16
PEER_HALF = 128
PEER_CHUNK = 1024
MIX_WIDTH = 3 * CONV_WIDTH + 2 * RNN_WIDTH + ATTN_WIDTH
RNN_X_BLOCK = 3 * CONV_WIDTH // RNN_WIDTH
RNN_GATE_BLOCK = RNN_X_BLOCK + 1
ATTN_Q_BLOCK = (3 * CONV_WIDTH + 2 * RNN_WIDTH) // ATTN_WIDTH
DEPTH = 1
DN_ALPHA = (2.0 * DEPTH) ** 0.25
LN_EPS = 1e-5

VMEM_LIMIT_V7X = 56 * 1024 * 1024
SUBLANES = 8
LANES = 128
BF16_ROWS = 2 * SUBLANES
NOT_SELECTED_RANK = 99.0


def _params(*sem):
    return pltpu.CompilerParams(dimension_semantics=sem, vmem_limit_bytes=VMEM_LIMIT_V7X)


def _resident(shape, index=None):
    index = (0,) * len(shape) if index is None else tuple(index)
    return pl.BlockSpec(shape, lambda *_: index, pipeline_mode=pl.Buffered(1))


def _dot(a, b):
    return jnp.dot(a, b, preferred_element_type=F32)


def _dot_nt(a, b):
    return lax.dot_general(a, b, (((1,), (1,)), ((), ())), preferred_element_type=F32)


def _sigmoid(x):
    return 1.0 / (1.0 + jnp.exp(-x))


def _gelu(x):
    return 0.5 * x * (1.0 + jnp.tanh(0.7978845608028654 * (x + 0.044715 * (x * x * x))))


def _softplus(x):
    return jnp.maximum(x, 0.0) + jnp.log1p(jnp.exp(-jnp.abs(x)))


def _layernorm(y, g, b):
    mu = jnp.mean(y, axis=-1, keepdims=True)
    yc = y - mu
    var = jnp.mean(yc * yc, axis=-1, keepdims=True)
    return yc * lax.rsqrt(var + LN_EPS) * g + b


def _mem_kv_kernel(m_ref, wk_ref, wv_ref, k_ref, v_ref):
    mb = m_ref[...].astype(BF16)
    k_ref[...] = _dot(mb, wk_ref[...])
    v_ref[...] = _dot(mb, wv_ref[...])


def _mem_kv(mem, wk, wv, tm=256):
    m = mem.shape[0]
    return pl.pallas_call(
        _mem_kv_kernel,
        grid=(m // tm,),
        in_specs=[pl.BlockSpec((tm, D_MODEL), lambda i: (i, 0)),
                  _resident((D_MODEL, ATTN_WIDTH)), _resident((D_MODEL, ATTN_WIDTH))],
        out_specs=[pl.BlockSpec((tm, ATTN_WIDTH), lambda i: (i, 0))] * 2,
        out_shape=[jax.ShapeDtypeStruct((m, ATTN_WIDTH), F32)] * 2,
        compiler_params=_params("arbitrary"),
        name="mem_kv",
    )(mem, wk, wv)


def _conv_kernel(x_ref, w_ref, cw_ref, y_ref, st_ref, pb_ref, *, ts):
    @pl.when(pl.program_id(1) == 0)
    def _():
        pb_ref[0:SUBLANES, :] = jnp.zeros((SUBLANES, CONV_WIDTH), F32)

    z = _dot(x_ref[0].astype(BF16), w_ref[...])
    p = z[:, CONV_WIDTH:2 * CONV_WIDTH] * z[:, 2 * CONV_WIDTH:]
    pb_ref[SUBLANES:SUBLANES + ts, :] = p
    p1 = pb_ref[SUBLANES - 1:SUBLANES - 1 + ts, :]
    p2 = pb_ref[SUBLANES - 2:SUBLANES - 2 + ts, :]
    cw = cw_ref[...]
    y = z[:, :CONV_WIDTH] * (cw[0:1] * p2 + cw[1:2] * p1 + cw[2:3] * p)
    y_ref[0] = y.astype(BF16)
    tail = pb_ref[ts:ts + SUBLANES, :]
    st_ref[0] = tail[SUBLANES - (CONV_K - 1):]
    pb_ref[0:SUBLANES, :] = tail


def _conv_branch(x, win, cw, ts=512):
    b, s, _ = x.shape
    return pl.pallas_call(
        functools.partial(_conv_kernel, ts=ts),
        grid=(b, s // ts),
        in_specs=[pl.BlockSpec((1, ts, D_MODEL), lambda i, j: (i, j, 0)),
                  _resident((D_MODEL, 3 * CONV_WIDTH)), _resident((CONV_K, CONV_WIDTH))],
        out_specs=[pl.BlockSpec((1, ts, CONV_WIDTH), lambda i, j: (i, j, 0)),
                   pl.BlockSpec((1, CONV_K - 1, CONV_WIDTH), lambda i, j: (i, 0, 0))],
        out_shape=[jax.ShapeDtypeStruct((b, s, CONV_WIDTH), BF16),
                   jax.ShapeDtypeStruct((b, CONV_K - 1, CONV_WIDTH), F32)],
        scratch_shapes=[pltpu.VMEM((SUBLANES + ts, CONV_WIDTH), F32)],
        compiler_params=_params("arbitrary", "arbitrary"),
        name="conv_branch",
    )(x, win, cw)


def _rglru_gates(xr, wg_ref, ba, bx, lam):
    xb = xr.astype(BF16)
    gr, gi = [], []
    for h in range(RNN_HEADS):
        g = _dot(xb[:, h * RNN_HEAD_DIM:(h + 1) * RNN_HEAD_DIM], wg_ref[h])
        gr.append(g[:, :RNN_HEAD_DIM])
        gi.append(g[:, RNN_HEAD_DIM:])
    r = _sigmoid(jnp.concatenate(gr, axis=1) + ba)
    i = _sigmoid(jnp.concatenate(gi, axis=1) + bx)
    log_a = -RG_C * r * _softplus(-lam)
    a = jnp.exp(log_a)
    th = jnp.tanh(log_a)
    w = -2.0 * th / (1.0 - th)
    u = jnp.where(w > 0.0, w * lax.rsqrt(w), 0.0) * (i * xr)
    return a, u


def _scan_rows(a, u, h_in, ts):
    pos = lax.broadcasted_iota(jnp.int32, a.shape, 0) % SUBLANES
    d = 1
    while d < SUBLANES:
        keep = pos >= d
        u = jnp.where(keep, u + a * pltpu.roll(u, d, axis=0), u)
        a = jnp.where(keep, a * pltpu.roll(a, d, axis=0), a)
        d *= 2
    groups = []
    h = h_in
    for g in range(ts // SUBLANES):
        rows = slice(g * SUBLANES, (g + 1) * SUBLANES)
        hg = a[rows] * h + u[rows]
        groups.append(hg)
        h = hg[SUBLANES - 1:SUBLANES]
    return jnp.concatenate(groups, axis=0)


def _rnn_kernel(x_ref, wx_ref, wgate_ref, cw_ref, cb_ref, wg_ref, ba_ref, bx_ref, lam_ref,
                y_ref, buf_ref, h_ref, rb_ref, hc_ref, *, ts):
    @pl.when(pl.program_id(1) == 0)
    def _():
        rb_ref[0:SUBLANES, :] = jnp.zeros((SUBLANES, RNN_WIDTH), F32)
        hc_ref[...] = jnp.zeros((1, RNN_WIDTH), F32)

    xb = x_ref[0].astype(BF16)
    rx = _dot(xb, wx_ref[...])
    rgate = _dot(xb, wgate_ref[...])
    rb_ref[SUBLANES:SUBLANES + ts, :] = rx
    cw = cw_ref[...]
    xr = cw[3:4] * rx + cb_ref[...]
    for k in range(RNN_CONV_K - 1):
        off = SUBLANES - (RNN_CONV_K - 1) + k
        xr = xr + cw[k:k + 1] * rb_ref[off:off + ts, :]
    a, u = _rglru_gates(xr, wg_ref, ba_ref[...], bx_ref[...], lam_ref[...])
    h = _scan_rows(a, u, hc_ref[...], ts)
    y_ref[0] = (h * _gelu(rgate)).astype(BF16)
    hc_ref[...] = h[ts - 1:ts, :]
    h_ref[0] = h[ts - 1:ts, :]
    tail = rb_ref[ts:ts + SUBLANES, :]
    buf_ref[0] = tail[SUBLANES - (RNN_CONV_K - 1):]
    rb_ref[0:SUBLANES, :] = tail


def _rnn_branch(x, win, cw, cb, wg, ba, bx, lam, ts=256):
    b, s, _ = x.shape
    row = _resident((1, RNN_WIDTH))
    return pl.pallas_call(
        functools.partial(_rnn_kernel, ts=ts),
        grid=(b, s // ts),
        in_specs=[pl.BlockSpec((1, ts, D_MODEL), lambda i, j: (i, j, 0)),
                  _resident((D_MODEL, RNN_WIDTH), (0, RNN_X_BLOCK)),
                  _resident((D_MODEL, RNN_WIDTH), (0, RNN_GATE_BLOCK)), _resident((RNN_CONV_K, RNN_WIDTH)), row,
                  _resident((RNN_HEADS, RNN_HEAD_DIM, 2 * RNN_HEAD_DIM)), row, row, row],
        out_specs=[pl.BlockSpec((1, ts, RNN_WIDTH), lambda i, j: (i, j, 0)),
                   pl.BlockSpec((1, RNN_CONV_K - 1, RNN_WIDTH), lambda i, j: (i, 0, 0)),
                   pl.BlockSpec((1, 1, RNN_WIDTH), lambda i, j: (i, 0, 0))],
        out_shape=[jax.ShapeDtypeStruct((b, s, RNN_WIDTH), BF16),
                   jax.ShapeDtypeStruct((b, RNN_CONV_K - 1, RNN_WIDTH), F32),
                   jax.ShapeDtypeStruct((b, 1, RNN_WIDTH), F32)],
        scratch_shapes=[pltpu.VMEM((SUBLANES + ts, RNN_WIDTH), F32), pltpu.VMEM((1, RNN_WIDTH), F32)],
        compiler_params=_params("arbitrary", "arbitrary"),
        name="rnn_branch",
    )(x, win, win, cw, cb, wg, ba, bx, lam)


def _attn_kernel(x_ref, w_ref, k_ref, v_ref, y_ref):
    q = _dot(x_ref[0].astype(BF16), w_ref[...]).astype(BF16)
    kb = k_ref[0].astype(BF16)
    vb = v_ref[0].astype(BF16)
    outs = []
    for h in range(ATTN_HEADS):
        sl = slice(h * ATTN_HEAD_DIM, (h + 1) * ATTN_HEAD_DIM)
        sc = _dot_nt(q[:, sl], kb[:, sl]) * (ATTN_HEAD_DIM ** -0.5)
        e = jnp.exp(sc - jnp.max(sc, axis=-1, keepdims=True))
        p = e / jnp.sum(e, axis=-1, keepdims=True)
        outs.append(_dot(p.astype(BF16), vb[:, sl]))
    y_ref[0] = jnp.concatenate(outs, axis=1).astype(BF16)


def _attn_branch(x, win, mk, mv, ts=512):
    b, s, _ = x.shape
    return pl.pallas_call(
        _attn_kernel,
        grid=(b, s // ts),
        in_specs=[pl.BlockSpec((1, ts, D_MODEL), lambda i, j: (i, j, 0)),
                  _resident((D_MODEL, ATTN_WIDTH), (0, ATTN_Q_BLOCK)),
                  pl.BlockSpec((1, N_MEM, ATTN_WIDTH), lambda i, j: (i, 0, 0)),
                  pl.BlockSpec((1, N_MEM, ATTN_WIDTH), lambda i, j: (i, 0, 0))],
        out_specs=pl.BlockSpec((1, ts, ATTN_WIDTH), lambda i, j: (i, j, 0)),
        out_shape=jax.ShapeDtypeStruct((b, s, ATTN_WIDTH), BF16),
        compiler_params=_params("arbitrary", "arbitrary"),
        name="attn_branch",
    )(x, win, mk, mv)


def _sample_seq_kernel(z_ref, cst_ref, rst_ref, h0_ref, ccw_ref, rcw_ref, rcb_ref, wg_ref,
                       ba_ref, bx_ref, lam_ref, yc_ref, yr_ref, cz_ref, rc_ref, h_ref):
    z = z_ref[...]
    p = z[:, CONV_WIDTH:2 * CONV_WIDTH] * z[:, 2 * CONV_WIDTH:3 * CONV_WIDTH]
    cst = cst_ref[...]
    ccw = ccw_ref[...]
    conv_y = ccw[0:1] * cst[:, 0, :] + ccw[1:2] * cst[:, 1, :] + ccw[2:3] * p
    yc_ref[...] = (z[:, :CONV_WIDTH] * conv_y).astype(BF16)
    cz_ref[:, 0, :] = cst[:, 1, :]
    cz_ref[:, 1, :] = p
    rx = z[:, 3 * CONV_WIDTH:3 * CONV_WIDTH + RNN_WIDTH]
    rst = rst_ref[...]
    rcw = rcw_ref[...]
    xr = rcw[0:1] * rst[:, 0, :] + rcw[1:2] * rst[:, 1, :] + rcw[2:3] * rst[:, 2, :] + rcw[3:4] * rx + rcb_ref[...]
    a, u = _rglru_gates(xr, wg_ref, ba_ref[...], bx_ref[...], lam_ref[...])
    h = a * h0_ref[...] + u
    yr_ref[...] = (h * _gelu(z[:, 3 * CONV_WIDTH + RNN_WIDTH:])).astype(BF16)
    h_ref[...] = h
    rc_ref[:, 0, :] = rst[:, 1, :]
    rc_ref[:, 1, :] = rst[:, 2, :]
    rc_ref[:, 2, :] = rx


def _sample_seq(z, cst, rst, h0, ccw, rcw, rcb, wg, ba, bx, lam, tb=32):
    n = z.shape[0]
    row = _resident((1, RNN_WIDTH))

    def tok(*tail):
        nd = len(tail)
        return pl.BlockSpec((tb,) + tail, lambda i: (i,) + (0,) * nd)

    return pl.pallas_call(
        _sample_seq_kernel,
        grid=(n // tb,),
        in_specs=[tok(3 * CONV_WIDTH + 2 * RNN_WIDTH), tok(CONV_K - 1, CONV_WIDTH),
                  tok(RNN_CONV_K - 1, RNN_WIDTH), tok(RNN_WIDTH),
                  _resident((CONV_K, CONV_WIDTH)), _resident((RNN_CONV_K, RNN_WIDTH)), row,
                  _resident((RNN_HEADS, RNN_HEAD_DIM, 2 * RNN_HEAD_DIM)), row, row, row],
        out_specs=[tok(CONV_WIDTH), tok(RNN_WIDTH),
                   tok(CONV_K - 1, CONV_WIDTH), tok(RNN_CONV_K - 1, RNN_WIDTH), tok(RNN_WIDTH)],
        out_shape=[jax.ShapeDtypeStruct((n, CONV_WIDTH), BF16), jax.ShapeDtypeStruct((n, RNN_WIDTH), BF16),
                   jax.ShapeDtypeStruct((n, CONV_K - 1, CONV_WIDTH), F32),
                   jax.ShapeDtypeStruct((n, RNN_CONV_K - 1, RNN_WIDTH), F32),
                   jax.ShapeDtypeStruct((n, RNN_WIDTH), F32)],
        compiler_params=_params("arbitrary"),
        name="sample_seq",
    )(z, cst, rst, h0, ccw, rcw, rcb, wg, ba, bx, lam)


def _sample_attn_kernel(q_ref, k_ref, v_ref, y_ref):
    q = q_ref[...]
    sc = jnp.sum(k_ref[0] * q[:, None], axis=-1, keepdims=True) * (ATTN_HEAD_DIM ** -0.5)
    e = jnp.exp(sc - jnp.max(sc, axis=1, keepdims=True))
    pr = e / jnp.sum(e, axis=1, keepdims=True)
    y_ref[...] = jnp.sum(pr * v_ref[0], axis=1).astype(BF16)


def _sample_attn(q, ck, cv, tb=2):
    n = q.shape[0]
    tok = pl.BlockSpec((tb, ATTN_HEADS, ATTN_HEAD_DIM), lambda i: (i, 0, 0))
    mem = pl.BlockSpec((1, tb, N_MEM, ATTN_HEADS, ATTN_HEAD_DIM), lambda i: (0, i, 0, 0, 0))
    return pl.pallas_call(
        _sample_attn_kernel,
        grid=(n // tb,),
        in_specs=[tok, mem, mem],
        out_specs=tok,
        out_shape=jax.ShapeDtypeStruct((n, ATTN_HEADS, ATTN_HEAD_DIM), BF16),
        compiler_params=_params("arbitrary"),
        name="sample_attn",
    )(q, ck, cv)


def _proj_kernel(x_ref, w_ref, o_ref):
    o_ref[...] = _dot(x_ref[...].astype(BF16), w_ref[...])


def _proj(x, w, n, tn=1024):
    m, k = x.shape
    return pl.pallas_call(
        _proj_kernel,
        grid=(n // tn,),
        in_specs=[_resident((m, k)), pl.BlockSpec((k, tn), lambda j: (0, j))],
        out_specs=pl.BlockSpec((m, tn), lambda j: (0, j)),
        out_shape=jax.ShapeDtypeStruct((m, n), F32),
        compiler_params=_params("arbitrary"),
        name="sample_proj",
    )(x, w)


def _gate_kernel(x_ref, yc_ref, yr_ref, ya_ref, wg0_ref, wg1_ref, wg2_ref, wc_ref, wr_ref, wa_ref, m_ref):
    xb = x_ref[...].astype(BF16)
    merged = (_sigmoid(_dot(xb, wg0_ref[...])) * _dot(yc_ref[...], wc_ref[...])
              + _sigmoid(_dot(xb, wg1_ref[...])) * _dot(yr_ref[...], wr_ref[...])
              + _sigmoid(_dot(xb, wg2_ref[...])) * _dot(ya_ref[...], wa_ref[...]))
    m_ref[...] = merged.astype(BF16)


def _out_proj_kernel(x_ref, m_ref, wo_ref, g_ref, b_ref, o_ref):
    o_ref[...] = _layernorm(DN_ALPHA * x_ref[...] + _dot(m_ref[...], wo_ref[...]), g_ref[...], b_ref[...])


def _merge(x, yc, yr, ya, win, wc, wr, wa, wo, g, b, tm, tn=512):
    t = x.shape[0]
    nb = D_MODEL // tn
    gate0 = (win.shape[1] - N_BRANCH * D_MODEL) // tn
    once = pl.Buffered(1)
    gate = [pl.BlockSpec((D_MODEL, tn), functools.partial(lambda n, i, k: (0, gate0 + k * nb + n), k=k),
                         pipeline_mode=once) for k in range(N_BRANCH)]
    br = pl.BlockSpec((CONV_WIDTH, tn), lambda n, i: (0, n), pipeline_mode=once)
    tok_h = pl.BlockSpec((tm, CONV_WIDTH), lambda n, i: (i, 0))
    merged = pl.pallas_call(
        _gate_kernel,
        grid=(nb, t // tm),
        in_specs=[pl.BlockSpec((tm, D_MODEL), lambda n, i: (i, 0)), tok_h, tok_h, tok_h] + gate + [br, br, br],
        out_specs=pl.BlockSpec((tm, tn), lambda n, i: (i, n)),
        out_shape=jax.ShapeDtypeStruct((t, D_MODEL), BF16),
        compiler_params=_params("arbitrary", "arbitrary"),
        name="gated_merge",
    )(x, yc, yr, ya, win, win, win, wc, wr, wa)
    tok = pl.BlockSpec((tm, D_MODEL), lambda i: (i, 0))
    row = _resident((1, D_MODEL))
    return pl.pallas_call(
        _out_proj_kernel,
        grid=(t // tm,),
        in_specs=[tok, tok, _resident((D_MODEL, D_MODEL)), row, row],
        out_specs=tok,
        out_shape=jax.ShapeDtypeStruct((t, D_MODEL), F32),
        compiler_params=_params("arbitrary"),
        name="out_proj",
    )(x, merged, wo, g, b)


def _top_rows(sc, k):
    n = sc.shape[0]
    rows = lax.broadcasted_iota(jnp.int32, sc.shape, 0)
    rank = jnp.full(sc.shape, NOT_SELECTED_RANK, F32)
    rem = sc
    vals = []
    for r in range(k):
        m = jnp.max(rem, axis=0, keepdims=True)
        first = jnp.min(jnp.where(rem == m, rows, n), axis=0, keepdims=True)
        hit = rows == first
        rank = jnp.where(hit, float(r), rank)
        rem = jnp.where(hit, -jnp.inf, rem)
        vals.append(m)
    return jnp.concatenate(vals, axis=0), rank


_WIDE_R1 = 8


def _candidates(v1, v2):
    blocks = [v1[0:1] + v2]
    blocks += [v1[r:r + 1] + v2[:SUBLANES] for r in range(1, _WIDE_R1)]
    blocks.append(v1[_WIDE_R1:] + v2[0:1])
    return jnp.concatenate(blocks, axis=0)


def _row_counts(chosen):
    cnt = jnp.where(chosen, 1.0, 0.0)
    counts = [jnp.sum(cnt[0:PEER_TOPK], axis=0, keepdims=True)]
    for r in range(1, _WIDE_R1):
        lo = PEER_TOPK + (r - 1) * SUBLANES
        counts.append(jnp.sum(cnt[lo:lo + SUBLANES], axis=0, keepdims=True))
    lo = PEER_TOPK + (_WIDE_R1 - 1) * SUBLANES
    counts += [cnt[lo + r:lo + r + 1] for r in range(PEER_TOPK - _WIDE_R1)]
    return counts


def _select_exact(s1, s2):
    v1, rank1 = _top_rows(s1, PEER_TOPK)
    v2, rank2 = _top_rows(s2, PEER_TOPK)
    cand = _candidates(v1, v2)
    _, crank = _top_rows(cand, PEER_TOPK)
    chosen = crank < NOT_SELECTED_RANK
    z = jnp.sum(jnp.where(chosen, jnp.exp(cand - cand[0:1]), 0.0), axis=0, keepdims=True)
    counts = _row_counts(chosen)
    n1 = jnp.zeros(s1.shape, F32)
    for r in range(PEER_TOPK):
        n1 = jnp.where(rank1 == float(r), counts[r], n1)
    return rank2, jnp.exp(s2 - v2[0:1]) / z, n1, jnp.exp(s1 - v1[0:1])


def _sorting_network(lo, hi):
    def merge(lo, hi, r):
        step = r * 2
        if step < hi - lo:
            yield from merge(lo, hi, step)
            yield from merge(lo + r, hi, step)
            yield from ((i, i + r) for i in range(lo + r, hi - r, step))
        else:
            yield (lo, lo + r)

    if hi - lo >= 1:
        mid = lo + (hi - lo) // 2
        yield from _sorting_network(lo, mid)
        yield from _sorting_network(mid + 1, hi)
        yield from merge(lo, hi, 1)


def _top_values(sc, k):
    assert sc.shape[0] == SUBLANES * k
    x = [sc[SUBLANES * v:SUBLANES * (v + 1)] for v in range(k)]
    for a, b in _sorting_network(0, k - 1):
        x[a], x[b] = jnp.maximum(x[a], x[b]), jnp.minimum(x[a], x[b])
    vals = []
    for r in range(k):
        m = jnp.max(x[0], axis=0, keepdims=True)
        vals.append(m)
        live = k - r
        if live > 1:
            hit = x[0] == m
            for p in range(live - 1):
                x[p] = jnp.where(hit, x[p + 1], x[p])
    return vals


def _count_at_least(sc, v):
    return jnp.sum(jnp.where(sc >= v, 1.0, 0.0), axis=0, keepdims=True)


def _top_pairs(v1, v2):
    lo = jnp.concatenate(v1[:_WIDE_R1], axis=0)
    x = [lo + v2[p] for p in range(PEER_TOPK)]
    tail = jnp.concatenate(v1[_WIDE_R1:], axis=0) + v2[0]
    top = v1[0] + v2[0]
    taken_lo = jnp.zeros_like(lo)
    taken_tail = jnp.zeros_like(lo)
    z = jnp.zeros_like(top)
    for r in range(PEER_TOPK):
        m = jnp.max(jnp.maximum(x[0], tail), axis=0, keepdims=True)
        hit, hit_tail = x[0] == m, tail == m
        taken_lo = taken_lo + jnp.where(hit, 1.0, 0.0)
        taken_tail = taken_tail + jnp.where(hit_tail, 1.0, 0.0)
        z = z + jnp.exp(m - top)
        live = PEER_TOPK - r
        if live > 1:
            for p in range(live - 1):
                x[p] = jnp.where(hit, x[p + 1], x[p])
            tail = jnp.where(hit_tail, -jnp.inf, tail)
    counts = [taken_lo[r:r + 1] for r in range(_WIDE_R1)]
    counts += [taken_tail[r:r + 1] for r in range(PEER_TOPK - _WIDE_R1)]
    return counts, z, jnp.sum(taken_lo + taken_tail, axis=0, keepdims=True)


def _select_distinct(s1, s2):
    v1 = _top_values(s1, PEER_TOPK)
    v2 = _top_values(s2, PEER_TOPK)
    counts, z, taken = _top_pairs(v1, v2)
    most = jnp.maximum(jnp.maximum(_count_at_least(s1, v1[-1]), _count_at_least(s2, v2[-1])), taken)
    n1 = jnp.zeros(s1.shape, F32)
    rank2 = jnp.full(s2.shape, NOT_SELECTED_RANK, F32)
    for r in range(PEER_TOPK):
        n1 = jnp.where(s1 == v1[r], counts[r], n1)
        rank2 = jnp.where(s2 == v2[r], float(r), rank2)
    return (rank2, jnp.exp(s2 - v2[0]) / z, n1, jnp.exp(s1 - v1[0])), most


def _peer_sel_kernel(x_ref, wq_ref, keys_ref, r2_ref, b2_ref, n1_ref, a1_ref, s_ref):
    q = _dot(x_ref[...].astype(BF16), wq_ref[...]).astype(BF16)
    ts = q.shape[0]

    def store(h, cols, sel):
        rank2, b2, n1, a1 = sel
        r2_ref[h, :, cols] = rank2.astype(BF16)
        b2_ref[h, :, cols] = b2.astype(BF16)
        n1_ref[h, :, cols] = n1
        a1_ref[h, :, cols] = a1

    for h in range(PEER_HEADS):
        s_ref[0] = _dot_nt(keys_ref[2 * h], q[:, (2 * h) * PEER_HALF:(2 * h + 1) * PEER_HALF])
        s_ref[1] = _dot_nt(keys_ref[2 * h + 1], q[:, (2 * h + 1) * PEER_HALF:(2 * h + 2) * PEER_HALF])
        most = None
        for c in range(ts // LANES):
            cols = slice(c * LANES, (c + 1) * LANES)
            sel, m = _select_distinct(s_ref[0, :, cols], s_ref[1, :, cols])
            store(h, cols, sel)
            most = m if most is None else jnp.maximum(most, m)

        @pl.when(jnp.max(most) > float(PEER_TOPK))
        def _():
            store(h, slice(None), _select_exact(s_ref[0], s_ref[1]))


def _peer_sel(x1, wq, keys, ts):
    t = x1.shape[0]
    sel = pl.BlockSpec((PEER_HEADS, PEER_NKEYS, ts), lambda i: (0, 0, i))
    shp = lambda dt: jax.ShapeDtypeStruct((PEER_HEADS, PEER_NKEYS, t), dt)
    r2, b2, n1, a1 = pl.pallas_call(
        _peer_sel_kernel,
        grid=(t // ts,),
        in_specs=[pl.BlockSpec((ts, D_MODEL), lambda i: (i, 0)),
                  _resident((D_MODEL, PEER_HEADS * 2 * PEER_HALF)),
                  _resident((PEER_HEADS * 2, PEER_NKEYS, PEER_HALF))],
        out_specs=[sel] * 4,
        out_shape=[shp(BF16), shp(BF16), shp(F32), shp(F32)],
        scratch_shapes=[pltpu.VMEM((2, PEER_NKEYS, ts), F32)],
        compiler_params=_params("arbitrary"),
        name="peer_sel",
    )(x1, wq, keys)
    grouped = (PEER_HEADS, PEER_NKEYS // BF16_ROWS, BF16_ROWS, t)
    return r2.reshape(grouped), b2.reshape(grouped), n1, a1


def _expert_weights(r2_ref, b2_ref, n1_ref, a1_ref, ii, tm):
    w = None
    for h in range(PEER_HEADS):
        nrow = jnp.broadcast_to(n1_ref[h, ii:ii + 1, :], (BF16_ROWS, tm)).astype(BF16)
        arow = jnp.broadcast_to(a1_ref[h, ii:ii + 1, :], (BF16_ROWS, tm)).astype(BF16)
        wh = jnp.where(r2_ref[h] < nrow[None], b2_ref[h], jnp.zeros((), BF16)) * arow[None]
        w = wh if w is None else w + wh
    return w.reshape(PEER_NKEYS, tm)


def _peer_ffn_kernel(x_ref, r2_ref, b2_ref, n1_ref, a1_ref, u_ref, vt_ref, g_ref, b_ref, o_ref,
                     xt_ref, act_ref, pt_ref, acc_ref, *, rows_per_chunk):
    c = pl.program_id(1)
    tm = x_ref.shape[0]
    piece = 2 * PEER_NKEYS

    @pl.when(c == 0)
    def _():
        xt_ref[...] = x_ref[...].T.astype(BF16)
        acc_ref[...] = jnp.zeros_like(acc_ref)

    def project(p):
        act_ref[p * piece:(p + 1) * piece, :] = _dot(u_ref[p * piece:(p + 1) * piece, :], xt_ref[...])

    def weigh(p):
        for ii in (2 * p, 2 * p + 1):
            rows = slice(ii * PEER_NKEYS, (ii + 1) * PEER_NKEYS)
            w = _expert_weights(r2_ref, b2_ref, n1_ref, a1_ref, ii, tm)
            pt_ref[rows, :] = w * _gelu(act_ref[rows, :]).astype(BF16)

    project(0)
    for p in range(1, rows_per_chunk // 2):
        project(p)
        weigh(p - 1)
    weigh(rows_per_chunk // 2 - 1)
    acc_ref[...] += _dot(vt_ref[...], pt_ref[...])

    @pl.when(c == pl.num_programs(1) - 1)
    def _():
        o_ref[...] = _layernorm(DN_ALPHA * x_ref[...] + acc_ref[...].T, g_ref[...], b_ref[...])


def _peer_ffn(x1, r2, b2, n1, a1, u, vt, g, b, tm):
    t = x1.shape[0]
    ec = PEER_CHUNK
    rows_per_chunk = ec // PEER_NKEYS
    tok = pl.BlockSpec((tm, D_MODEL), lambda i, c: (i, 0))
    col = pl.BlockSpec((PEER_HEADS, PEER_NKEYS // BF16_ROWS, BF16_ROWS, tm), lambda i, c: (0, 0, 0, i))
    chunk_rows = pl.BlockSpec((PEER_HEADS, rows_per_chunk, tm), lambda i, c: (0, c, i))
    row = _resident((1, D_MODEL))
    return pl.pallas_call(
        functools.partial(_peer_ffn_kernel, rows_per_chunk=rows_per_chunk),
        grid=(t // tm, PEER_EXPERTS // ec),
        in_specs=[tok, col, col, chunk_rows, chunk_rows,
                  pl.BlockSpec((ec, D_MODEL), lambda i, c: (c, 0)),
                  pl.BlockSpec((D_MODEL, ec), lambda i, c: (0, c)), row, row],
        out_specs=tok,
        out_shape=jax.ShapeDtypeStruct((t, D_MODEL), F32),
        scratch_shapes=[pltpu.VMEM((D_MODEL, tm), BF16), pltpu.VMEM((ec, tm), F32), pltpu.VMEM((ec, tm), BF16),
                        pltpu.VMEM((D_MODEL, tm), F32)],
        compiler_params=_params("arbitrary", "arbitrary"),
        name="peer_ffn",
    )(x1, r2, b2, n1, a1, u, vt, g, b)


def _cast_t_kernel(v_ref, o_ref):
    o_ref[...] = v_ref[...].T.astype(BF16)


def _cast_transpose(v, tr=512):
    rows, cols = v.shape
    return pl.pallas_call(
        _cast_t_kernel,
        grid=(rows // tr,),
        in_specs=[pl.BlockSpec((tr, cols), lambda i: (i, 0))],
        out_specs=pl.BlockSpec((cols, tr), lambda i: (0, i)),
        out_shape=jax.ShapeDtypeStruct((cols, rows), BF16),
        compiler_params=_params("arbitrary"),
        name="cast_transpose",
    )(v)


def _channel_sublayers(x, yc, yr, ya, wts, tm):
    x1 = _merge(x, yc, yr, ya, wts["in"], wts["br_conv"], wts["br_rnn"], wts["br_attn"], wts["o"],
                wts["ln1_g"], wts["ln1_b"], tm)
    r2, b2, n1, a1 = _peer_sel(x1, wts["peer_wq"], wts["peer_keys"], min(tm, 256))
    return _peer_ffn(x1, r2, b2, n1, a1, wts["peer_u"], wts["peer_vt"], wts["ln2_g"], wts["ln2_b"], tm)


def kernel(x_prompt, x_sample, mem_prompt, cache_mem_k, cache_mem_v, state_conv_z, state_rglru_conv, state_rglru_h, w_in, conv_w, rg_conv_w, rg_conv_b, rg_wa, rg_ba, rg_wx, rg_bx, rg_lambda, w_mk, w_mv, w_br_conv, w_br_rnn, w_br_attn, w_o, ln1_g, ln1_b, peer_wq, peer_keys, peer_u, peer_v, ln2_g, ln2_b):
    assert w_in.shape[0] == DEPTH == 1
    bp, sp, _ = x_prompt.shape
    bs = x_sample.shape[0]
    l = 0
    seq_width = MIX_WIDTH - ATTN_WIDTH
    win = w_in[l].astype(BF16)
    row = lambda a: a[l].reshape(1, -1)
    wg = jnp.concatenate([rg_wa[l], rg_wx[l]], axis=-1).astype(BF16)
    wts = {
        "in": win,
        "br_conv": w_br_conv[l].astype(BF16), "br_rnn": w_br_rnn[l].astype(BF16),
        "br_attn": w_br_attn[l].astype(BF16), "o": w_o[l].astype(BF16),
        "ln1_g": row(ln1_g), "ln1_b": row(ln1_b), "ln2_g": row(ln2_g), "ln2_b": row(ln2_b),
        "peer_wq": peer_wq[l].astype(BF16),
        "peer_keys": peer_keys[l].reshape(PEER_HEADS * 2, PEER_NKEYS, PEER_HALF).astype(BF16),
        "peer_u": peer_u[l].astype(BF16), "peer_vt": _cast_transpose(peer_v[l]),
    }
    rnn_args = (rg_conv_w[l], row(rg_conv_b), wg, row(rg_ba), row(rg_bx), row(rg_lambda))

    mk, mv = _mem_kv(mem_prompt.reshape(bp * N_MEM, D_MODEL), w_mk[l].astype(BF16), w_mv[l].astype(BF16))
    yc_p, cz_p = _conv_branch(x_prompt, win, conv_w[l])
    yr_p, rc_p, h_p = _rnn_branch(x_prompt, win, *rnn_args)
    ya_p = _attn_branch(x_prompt, win, mk.reshape(bp, N_MEM, ATTN_WIDTH), mv.reshape(bp, N_MEM, ATTN_WIDTH))
    tp = bp * sp
    y_p = _channel_sublayers(x_prompt.reshape(tp, D_MODEL), yc_p.reshape(tp, CONV_WIDTH),
                             yr_p.reshape(tp, RNN_WIDTH), ya_p.reshape(tp, ATTN_WIDTH), wts, 512)

    xs = x_sample.reshape(bs, D_MODEL)
    z_s = _proj(xs, win, MIX_WIDTH)
    yc_s, yr_s, cz_s, rc_s, h_s = _sample_seq(
        z_s, state_conv_z[l], state_rglru_conv[l], state_rglru_h[l], conv_w[l], *rnn_args)
    ya_s = _sample_attn(z_s[:, seq_width:].reshape(bs, ATTN_HEADS, ATTN_HEAD_DIM), cache_mem_k, cache_mem_v)
    ya_s = ya_s.reshape(bs, ATTN_WIDTH)
    y_s = _channel_sublayers(xs, yc_s, yr_s, ya_s, wts, bs)

    hd = (ATTN_HEADS, ATTN_HEAD_DIM)
    return (y_p.reshape(bp, sp, D_MODEL), y_s.reshape(bs, 1, D_MODEL),
            mk.reshape(1, bp, N_MEM, *hd), mv.reshape(1, bp, N_MEM, *hd),
            cz_p[None], rc_p[None], h_p.reshape(1, bp, RNN_WIDTH),
            cz_s[None], rc_s[None], h_s[None])
```

```python
import functools

import jax
import jax.numpy as jnp
from jax import lax
from jax.experimental import pallas as pl
from jax.experimental.pallas import tpu as pltpu

F32 = jnp.float32
BF16 = jnp.bfloat16

D_MODEL = 2048
N_MEM = 256
CONV_WIDTH = 1024
CONV_K = 3
RNN_WIDTH = 1024
RNN_HEADS = 8
RNN_HEAD_DIM = RNN_WIDTH // RNN_HEADS
RNN_CONV_K = 4
RG_C = 8.0
ATTN_HEADS = 4
ATTN_HEAD_DIM = 256
ATTN_WIDTH = ATTN_HEADS * ATTN_HEAD_DIM
N_BRANCH = 3
PEER_HEADS = 8
PEER_NKEYS = 128
PEER_EXPERTS = PEER_NKEYS * PEER_NKEYS
PEER_TOPK = 16
PEER_HALF = 128
PEER_CHUNK = 1024
PIECE_ROWS = (4, 4)
MIX_WIDTH = 3 * CONV_WIDTH + 2 * RNN_WIDTH + ATTN_WIDTH
RNN_X_BLOCK = 3 * CONV_WIDTH // RNN_WIDTH
RNN_GATE_BLOCK = RNN_X_BLOCK + 1
ATTN_Q_BLOCK = (3 * CONV_WIDTH + 2 * RNN_WIDTH) // ATTN_WIDTH
DEPTH = 1
DN_ALPHA = (2.0 * DEPTH) ** 0.25
LN_EPS = 1e-5

VMEM_LIMIT_V7X = 56 * 1024 * 1024
SUBLANES = 8
LANES = 128
BF16_ROWS = 2 * SUBLANES
NOT_SELECTED_RANK = 99.0


def _params(*sem):
    return pltpu.CompilerParams(dimension_semantics=sem, vmem_limit_bytes=VMEM_LIMIT_V7X)


def _resident(shape, index=None):
    index = (0,) * len(shape) if index is None else tuple(index)
    return pl.BlockSpec(shape, lambda *_: index, pipeline_mode=pl.Buffered(1))


def _dot(a, b):
    return jnp.dot(a, b, preferred_element_type=F32)


def _dot_nt(a, b):
    return lax.dot_general(a, b, (((1,), (1,)), ((), ())), preferred_element_type=F32)


def _sigmoid(x):
    return 1.0 / (1.0 + jnp.exp(-x))


def _gelu(x):
    return 0.5 * x * (1.0 + jnp.tanh(0.7978845608028654 * (x + 0.044715 * (x * x * x))))


def _softplus(x):
    return jnp.maximum(x, 0.0) + jnp.log1p(jnp.exp(-jnp.abs(x)))


def _layernorm(y, g, b):
    mu = jnp.mean(y, axis=-1, keepdims=True)
    yc = y - mu
    var = jnp.mean(yc * yc, axis=-1, keepdims=True)
    return yc * lax.rsqrt(var + LN_EPS) * g + b


def _mem_kv_kernel(m_ref, wk_ref, wv_ref, k_ref, v_ref):
    mb = m_ref[...].astype(BF16)
    k_ref[...] = _dot(mb, wk_ref[...])
    v_ref[...] = _dot(mb, wv_ref[...])


def _mem_kv(mem, wk, wv, tm=256):
    m = mem.shape[0]
    return pl.pallas_call(
        _mem_kv_kernel,
        grid=(m // tm,),
        in_specs=[pl.BlockSpec((tm, D_MODEL), lambda i: (i, 0)),
                  _resident((D_MODEL, ATTN_WIDTH)), _resident((D_MODEL, ATTN_WIDTH))],
        out_specs=[pl.BlockSpec((tm, ATTN_WIDTH), lambda i: (i, 0))] * 2,
        out_shape=[jax.ShapeDtypeStruct((m, ATTN_WIDTH), F32)] * 2,
        compiler_params=_params("arbitrary"),
        name="mem_kv",
    )(mem, wk, wv)


def _conv_kernel(x_ref, w_ref, cw_ref, y_ref, st_ref, pb_ref, *, ts):
    @pl.when(pl.program_id(1) == 0)
    def _():
        pb_ref[0:SUBLANES, :] = jnp.zeros((SUBLANES, CONV_WIDTH), F32)

    z = _dot(x_ref[0].astype(BF16), w_ref[...])
    p = z[:, CONV_WIDTH:2 * CONV_WIDTH] * z[:, 2 * CONV_WIDTH:]
    pb_ref[SUBLANES:SUBLANES + ts, :] = p
    p1 = pb_ref[SUBLANES - 1:SUBLANES - 1 + ts, :]
    p2 = pb_ref[SUBLANES - 2:SUBLANES - 2 + ts, :]
    cw = cw_ref[...]
    y = z[:, :CONV_WIDTH] * (cw[0:1] * p2 + cw[1:2] * p1 + cw[2:3] * p)
    y_ref[0] = y.astype(BF16)
    tail = pb_ref[ts:ts + SUBLANES, :]
    st_ref[0] = tail[SUBLANES - (CONV_K - 1):]
    pb_ref[0:SUBLANES, :] = tail


def _conv_branch(x, win, cw, ts=512):
    b, s, _ = x.shape
    return pl.pallas_call(
        functools.partial(_conv_kernel, ts=ts),
        grid=(b, s // ts),
        in_specs=[pl.BlockSpec((1, ts, D_MODEL), lambda i, j: (i, j, 0)),
                  _resident((D_MODEL, 3 * CONV_WIDTH)), _resident((CONV_K, CONV_WIDTH))],
        out_specs=[pl.BlockSpec((1, ts, CONV_WIDTH), lambda i, j: (i, j, 0)),
                   pl.BlockSpec((1, CONV_K - 1, CONV_WIDTH), lambda i, j: (i, 0, 0))],
        out_shape=[jax.ShapeDtypeStruct((b, s, CONV_WIDTH), BF16),
                   jax.ShapeDtypeStruct((b, CONV_K - 1, CONV_WIDTH), F32)],
        scratch_shapes=[pltpu.VMEM((SUBLANES + ts, CONV_WIDTH), F32)],
        compiler_params=_params("arbitrary", "arbitrary"),
        name="conv_branch",
    )(x, win, cw)


def _rglru_gates(xr, wg_ref, ba, bx, lam):
    xb = xr.astype(BF16)
    gr, gi = [], []
    for h in range(RNN_HEADS):
        g = _dot(xb[:, h * RNN_HEAD_DIM:(h + 1) * RNN_HEAD_DIM], wg_ref[h])
        gr.append(g[:, :RNN_HEAD_DIM])
        gi.append(g[:, RNN_HEAD_DIM:])
    r = _sigmoid(jnp.concatenate(gr, axis=1) + ba)
    i = _sigmoid(jnp.concatenate(gi, axis=1) + bx)
    log_a = -RG_C * r * _softplus(-lam)
    a = jnp.exp(log_a)
    th = jnp.tanh(log_a)
    w = -2.0 * th / (1.0 - th)
    u = jnp.where(w > 0.0, w * lax.rsqrt(w), 0.0) * (i * xr)
    return a, u


def _scan_rows(a, u, h_in, ts):
    pos = lax.broadcasted_iota(jnp.int32, a.shape, 0) % SUBLANES
    d = 1
    while d < SUBLANES:
        keep = pos >= d
        u = jnp.where(keep, u + a * pltpu.roll(u, d, axis=0), u)
        a = jnp.where(keep, a * pltpu.roll(a, d, axis=0), a)
        d *= 2
    groups = []
    h = h_in
    for g in range(ts // SUBLANES):
        rows = slice(g * SUBLANES, (g + 1) * SUBLANES)
        hg = a[rows] * h + u[rows]
        groups.append(hg)
        h = hg[SUBLANES - 1:SUBLANES]
    return jnp.concatenate(groups, axis=0)


def _rnn_kernel(x_ref, wx_ref, wgate_ref, cw_ref, cb_ref, wg_ref, ba_ref, bx_ref, lam_ref,
                y_ref, buf_ref, h_ref, rb_ref, hc_ref, *, ts):
    @pl.when(pl.program_id(1) == 0)
    def _():
        rb_ref[0:SUBLANES, :] = jnp.zeros((SUBLANES, RNN_WIDTH), F32)
        hc_ref[...] = jnp.zeros((1, RNN_WIDTH), F32)

    xb = x_ref[0].astype(BF16)
    rx = _dot(xb, wx_ref[...])
    rgate = _dot(xb, wgate_ref[...])
    rb_ref[SUBLANES:SUBLANES + ts, :] = rx
    cw = cw_ref[...]
    xr = cw[3:4] * rx + cb_ref[...]
    for k in range(RNN_CONV_K - 1):
        off = SUBLANES - (RNN_CONV_K - 1) + k
        xr = xr + cw[k:k + 1] * rb_ref[off:off + ts, :]
    a, u = _rglru_gates(xr, wg_ref, ba_ref[...], bx_ref[...], lam_ref[...])
    h = _scan_rows(a, u, hc_ref[...], ts)
    y_ref[0] = (h * _gelu(rgate)).astype(BF16)
    hc_ref[...] = h[ts - 1:ts, :]
    h_ref[0] = h[ts - 1:ts, :]
    tail = rb_ref[ts:ts + SUBLANES, :]
    buf_ref[0] = tail[SUBLANES - (RNN_CONV_K - 1):]
    rb_ref[0:SUBLANES, :] = tail


def _rnn_branch(x, win, cw, cb, wg, ba, bx, lam, ts=512):
    b, s, _ = x.shape
    row = _resident((1, RNN_WIDTH))
    return pl.pallas_call(
        functools.partial(_rnn_kernel, ts=ts),
        grid=(b, s // ts),
        in_specs=[pl.BlockSpec((1, ts, D_MODEL), lambda i, j: (i, j, 0)),
                  _resident((D_MODEL, RNN_WIDTH), (0, RNN_X_BLOCK)),
                  _resident((D_MODEL, RNN_WIDTH), (0, RNN_GATE_BLOCK)), _resident((RNN_CONV_K, RNN_WIDTH)), row,
                  _resident((RNN_HEADS, RNN_HEAD_DIM, 2 * RNN_HEAD_DIM)), row, row, row],
        out_specs=[pl.BlockSpec((1, ts, RNN_WIDTH), lambda i, j: (i, j, 0)),
                   pl.BlockSpec((1, RNN_CONV_K - 1, RNN_WIDTH), lambda i, j: (i, 0, 0)),
                   pl.BlockSpec((1, 1, RNN_WIDTH), lambda i, j: (i, 0, 0))],
        out_shape=[jax.ShapeDtypeStruct((b, s, RNN_WIDTH), BF16),
                   jax.ShapeDtypeStruct((b, RNN_CONV_K - 1, RNN_WIDTH), F32),
                   jax.ShapeDtypeStruct((b, 1, RNN_WIDTH), F32)],
        scratch_shapes=[pltpu.VMEM((SUBLANES + ts, RNN_WIDTH), F32), pltpu.VMEM((1, RNN_WIDTH), F32)],
        compiler_params=_params("arbitrary", "arbitrary"),
        name="rnn_branch",
    )(x, win, win, cw, cb, wg, ba, bx, lam)


def _attn_kernel(x_ref, w_ref, k_ref, v_ref, y_ref):
    q = _dot(x_ref[0].astype(BF16), w_ref[...]).astype(BF16)
    kb = k_ref[0].astype(BF16)
    vb = v_ref[0].astype(BF16)
    outs = []
    for h in range(ATTN_HEADS):
        sl = slice(h * ATTN_HEAD_DIM, (h + 1) * ATTN_HEAD_DIM)
        sc = _dot_nt(q[:, sl], kb[:, sl]) * (ATTN_HEAD_DIM ** -0.5)
        e = jnp.exp(sc - jnp.max(sc, axis=-1, keepdims=True))
        p = e / jnp.sum(e, axis=-1, keepdims=True)
        outs.append(_dot(p.astype(BF16), vb[:, sl]))
    y_ref[0] = jnp.concatenate(outs, axis=1).astype(BF16)


def _attn_branch(x, win, mk, mv, ts=512):
    b, s, _ = x.shape
    return pl.pallas_call(
        _attn_kernel,
        grid=(b, s // ts),
        in_specs=[pl.BlockSpec((1, ts, D_MODEL), lambda i, j: (i, j, 0)),
                  _resident((D_MODEL, ATTN_WIDTH), (0, ATTN_Q_BLOCK)),
                  pl.BlockSpec((1, N_MEM, ATTN_WIDTH), lambda i, j: (i, 0, 0)),
                  pl.BlockSpec((1, N_MEM, ATTN_WIDTH), lambda i, j: (i, 0, 0))],
        out_specs=pl.BlockSpec((1, ts, ATTN_WIDTH), lambda i, j: (i, j, 0)),
        out_shape=jax.ShapeDtypeStruct((b, s, ATTN_WIDTH), BF16),
        compiler_params=_params("arbitrary", "arbitrary"),
        name="attn_branch",
    )(x, win, mk, mv)


def _sample_seq_kernel(z_ref, cst_ref, rst_ref, h0_ref, ccw_ref, rcw_ref, rcb_ref, wg_ref,
                       ba_ref, bx_ref, lam_ref, yc_ref, yr_ref, cz_ref, rc_ref, h_ref):
    z = z_ref[...]
    p = z[:, CONV_WIDTH:2 * CONV_WIDTH] * z[:, 2 * CONV_WIDTH:3 * CONV_WIDTH]
    cst = cst_ref[...]
    ccw = ccw_ref[...]
    conv_y = ccw[0:1] * cst[:, 0, :] + ccw[1:2] * cst[:, 1, :] + ccw[2:3] * p
    yc_ref[...] = (z[:, :CONV_WIDTH] * conv_y).astype(BF16)
    cz_ref[:, 0, :] = cst[:, 1, :]
    cz_ref[:, 1, :] = p
    rx = z[:, 3 * CONV_WIDTH:3 * CONV_WIDTH + RNN_WIDTH]
    rst = rst_ref[...]
    rcw = rcw_ref[...]
    xr = rcw[0:1] * rst[:, 0, :] + rcw[1:2] * rst[:, 1, :] + rcw[2:3] * rst[:, 2, :] + rcw[3:4] * rx + rcb_ref[...]
    a, u = _rglru_gates(xr, wg_ref, ba_ref[...], bx_ref[...], lam_ref[...])
    h = a * h0_ref[...] + u
    yr_ref[...] = (h * _gelu(z[:, 3 * CONV_WIDTH + RNN_WIDTH:])).astype(BF16)
    h_ref[...] = h
    rc_ref[:, 0, :] = rst[:, 1, :]
    rc_ref[:, 1, :] = rst[:, 2, :]
    rc_ref[:, 2, :] = rx


def _sample_seq(z, cst, rst, h0, ccw, rcw, rcb, wg, ba, bx, lam, tb=32):
    n = z.shape[0]
    row = _resident((1, RNN_WIDTH))

    def tok(*tail):
        nd = len(tail)
        return pl.BlockSpec((tb,) + tail, lambda i: (i,) + (0,) * nd)

    return pl.pallas_call(
        _sample_seq_kernel,
        grid=(n // tb,),
        in_specs=[tok(3 * CONV_WIDTH + 2 * RNN_WIDTH), tok(CONV_K - 1, CONV_WIDTH),
                  tok(RNN_CONV_K - 1, RNN_WIDTH), tok(RNN_WIDTH),
                  _resident((CONV_K, CONV_WIDTH)), _resident((RNN_CONV_K, RNN_WIDTH)), row,
                  _resident((RNN_HEADS, RNN_HEAD_DIM, 2 * RNN_HEAD_DIM)), row, row, row],
        out_specs=[tok(CONV_WIDTH), tok(RNN_WIDTH),
                   tok(CONV_K - 1, CONV_WIDTH), tok(RNN_CONV_K - 1, RNN_WIDTH), tok(RNN_WIDTH)],
        out_shape=[jax.ShapeDtypeStruct((n, CONV_WIDTH), BF16), jax.ShapeDtypeStruct((n, RNN_WIDTH), BF16),
                   jax.ShapeDtypeStruct((n, CONV_K - 1, CONV_WIDTH), F32),
                   jax.ShapeDtypeStruct((n, RNN_CONV_K - 1, RNN_WIDTH), F32),
                   jax.ShapeDtypeStruct((n, RNN_WIDTH), F32)],
        compiler_params=_params("arbitrary"),
        name="sample_seq",
    )(z, cst, rst, h0, ccw, rcw, rcb, wg, ba, bx, lam)


def _sample_attn_kernel(q_ref, k_ref, v_ref, y_ref):
    q = q_ref[...]
    sc = jnp.sum(k_ref[0] * q[:, None], axis=-1, keepdims=True) * (ATTN_HEAD_DIM ** -0.5)
    e = jnp.exp(sc - jnp.max(sc, axis=1, keepdims=True))
    pr = e / jnp.sum(e, axis=1, keepdims=True)
    y_ref[...] = jnp.sum(pr * v_ref[0], axis=1).astype(BF16)


def _sample_attn(q, ck, cv, tb=4):
    n = q.shape[0]
    tok = pl.BlockSpec((tb, ATTN_HEADS, ATTN_HEAD_DIM), lambda i: (i, 0, 0))
    mem = pl.BlockSpec((1, tb, N_MEM, ATTN_HEADS, ATTN_HEAD_DIM), lambda i: (0, i, 0, 0, 0))
    return pl.pallas_call(
        _sample_attn_kernel,
        grid=(n // tb,),
        in_specs=[tok, mem, mem],
        out_specs=tok,
        out_shape=jax.ShapeDtypeStruct((n, ATTN_HEADS, ATTN_HEAD_DIM), BF16),
        compiler_params=_params("arbitrary"),
        name="sample_attn",
    )(q, ck, cv)


def _proj_kernel(x_ref, w_ref, o_ref):
    o_ref[...] = _dot(x_ref[...].astype(BF16), w_ref[...])


def _proj(x, w, n, tn=1024):
    m, k = x.shape
    return pl.pallas_call(
        _proj_kernel,
        grid=(n // tn,),
        in_specs=[_resident((m, k)), pl.BlockSpec((k, tn), lambda j: (0, j))],
        out_specs=pl.BlockSpec((m, tn), lambda j: (0, j)),
        out_shape=jax.ShapeDtypeStruct((m, n), F32),
        compiler_params=_params("arbitrary"),
        name="sample_proj",
    )(x, w)


def _gate_kernel(x_ref, yc_ref, yr_ref, ya_ref, wg0_ref, wg1_ref, wg2_ref, wc_ref, wr_ref, wa_ref, m_ref):
    xb = x_ref[...].astype(BF16)
    merged = (_sigmoid(_dot(xb, wg0_ref[...])) * _dot(yc_ref[...], wc_ref[...])
              + _sigmoid(_dot(xb, wg1_ref[...])) * _dot(yr_ref[...], wr_ref[...])
              + _sigmoid(_dot(xb, wg2_ref[...])) * _dot(ya_ref[...], wa_ref[...]))
    m_ref[...] = merged.astype(BF16)


def _out_proj_kernel(x_ref, m_ref, wo_ref, g_ref, b_ref, o_ref):
    o_ref[...] = _layernorm(DN_ALPHA * x_ref[...] + _dot(m_ref[...], wo_ref[...]), g_ref[...], b_ref[...])


def _merge(x, yc, yr, ya, win, wc, wr, wa, wo, g, b, tm, tn=1024):
    t = x.shape[0]
    nb = D_MODEL // tn
    gate0 = (win.shape[1] - N_BRANCH * D_MODEL) // tn
    once = pl.Buffered(1)
    gate = [pl.BlockSpec((D_MODEL, tn), functools.partial(lambda n, i, k: (0, gate0 + k * nb + n), k=k),
                         pipeline_mode=once) for k in range(N_BRANCH)]
    br = pl.BlockSpec((CONV_WIDTH, tn), lambda n, i: (0, n), pipeline_mode=once)
    tok_h = pl.BlockSpec((tm, CONV_WIDTH), lambda n, i: (i, 0))
    merged = pl.pallas_call(
        _gate_kernel,
        grid=(nb, t // tm),
        in_specs=[pl.BlockSpec((tm, D_MODEL), lambda n, i: (i, 0)), tok_h, tok_h, tok_h] + gate + [br, br, br],
        out_specs=pl.BlockSpec((tm, tn), lambda n, i: (i, n)),
        out_shape=jax.ShapeDtypeStruct((t, D_MODEL), BF16),
        compiler_params=_params("arbitrary", "arbitrary"),
        name="gated_merge",
    )(x, yc, yr, ya, win, win, win, wc, wr, wa)
    tok = pl.BlockSpec((tm, D_MODEL), lambda i: (i, 0))
    row = _resident((1, D_MODEL))
    return pl.pallas_call(
        _out_proj_kernel,
        grid=(t // tm,),
        in_specs=[tok, tok, _resident((D_MODEL, D_MODEL)), row, row],
        out_specs=tok,
        out_shape=jax.ShapeDtypeStruct((t, D_MODEL), F32),
        compiler_params=_params("arbitrary"),
        name="out_proj",
    )(x, merged, wo, g, b)


def _top_rows(sc, k):
    n = sc.shape[0]
    rows = lax.broadcasted_iota(jnp.int32, sc.shape, 0)
    rank = jnp.full(sc.shape, NOT_SELECTED_RANK, F32)
    rem = sc
    vals = []
    for r in range(k):
        m = jnp.max(rem, axis=0, keepdims=True)
        first = jnp.min(jnp.where(rem == m, rows, n), axis=0, keepdims=True)
        hit = rows == first
        rank = jnp.where(hit, float(r), rank)
        rem = jnp.where(hit, -jnp.inf, rem)
        vals.append(m)
    return jnp.concatenate(vals, axis=0), rank


_WIDE_R1 = 8


def _candidates(v1, v2):
    blocks = [v1[0:1] + v2]
    blocks += [v1[r:r + 1] + v2[:SUBLANES] for r in range(1, _WIDE_R1)]
    blocks.append(v1[_WIDE_R1:] + v2[0:1])
    return jnp.concatenate(blocks, axis=0)


def _row_counts(chosen):
    cnt = jnp.where(chosen, 1.0, 0.0)
    counts = [jnp.sum(cnt[0:PEER_TOPK], axis=0, keepdims=True)]
    for r in range(1, _WIDE_R1):
        lo = PEER_TOPK + (r - 1) * SUBLANES
        counts.append(jnp.sum(cnt[lo:lo + SUBLANES], axis=0, keepdims=True))
    lo = PEER_TOPK + (_WIDE_R1 - 1) * SUBLANES
    counts += [cnt[lo + r:lo + r + 1] for r in range(PEER_TOPK - _WIDE_R1)]
    return counts


def _select_exact(s1, s2):
    v1, rank1 = _top_rows(s1, PEER_TOPK)
    v2, rank2 = _top_rows(s2, PEER_TOPK)
    cand = _candidates(v1, v2)
    _, crank = _top_rows(cand, PEER_TOPK)
    chosen = crank < NOT_SELECTED_RANK
    z = jnp.sum(jnp.where(chosen, jnp.exp(cand - cand[0:1]), 0.0), axis=0, keepdims=True)
    counts = _row_counts(chosen)
    n1 = jnp.zeros(s1.shape, F32)
    for r in range(PEER_TOPK):
        n1 = jnp.where(rank1 == float(r), counts[r], n1)
    return rank2, jnp.exp(s2 - v2[0:1]) / z, n1, jnp.exp(s1 - v1[0:1])


def _sorting_network(lo, hi):
    def merge(lo, hi, r):
        step = r * 2
        if step < hi - lo:
            yield from merge(lo, hi, step)
            yield from merge(lo + r, hi, step)
            yield from ((i, i + r) for i in range(lo + r, hi - r, step))
        else:
            yield (lo, lo + r)

    if hi - lo >= 1:
        mid = lo + (hi - lo) // 2
        yield from _sorting_network(lo, mid)
        yield from _sorting_network(mid + 1, hi)
        yield from merge(lo, hi, 1)


def _top_values(sc, k):
    assert sc.shape[0] == SUBLANES * k
    x = [sc[SUBLANES * v:SUBLANES * (v + 1)] for v in range(k)]
    for a, b in _sorting_network(0, k - 1):
        x[a], x[b] = jnp.maximum(x[a], x[b]), jnp.minimum(x[a], x[b])
    vals = []
    for r in range(k):
        m = jnp.max(x[0], axis=0, keepdims=True)
        vals.append(m)
        live = k - r
        if live > 1:
            hit = x[0] == m
            for p in range(live - 1):
                x[p] = jnp.where(hit, x[p + 1], x[p])
    return vals


def _count_at_least(sc, v):
    return jnp.sum(jnp.where(sc >= v, 1.0, 0.0), axis=0, keepdims=True)


def _top_pairs(v1, v2):
    lo = jnp.concatenate(v1[:_WIDE_R1], axis=0)
    x = [lo + v2[p] for p in range(PEER_TOPK)]
    tail = jnp.concatenate(v1[_WIDE_R1:], axis=0) + v2[0]
    top = v1[0] + v2[0]
    taken_lo = jnp.zeros_like(lo)
    taken_tail = jnp.zeros_like(lo)
    z = jnp.zeros_like(top)
    for r in range(PEER_TOPK):
        m = jnp.max(jnp.maximum(x[0], tail), axis=0, keepdims=True)
        hit, hit_tail = x[0] == m, tail == m
        taken_lo = taken_lo + jnp.where(hit, 1.0, 0.0)
        taken_tail = taken_tail + jnp.where(hit_tail, 1.0, 0.0)
        z = z + jnp.exp(m - top)
        live = PEER_TOPK - r
        if live > 1:
            for p in range(live - 1):
                x[p] = jnp.where(hit, x[p + 1], x[p])
            tail = jnp.where(hit_tail, -jnp.inf, tail)
    counts = [taken_lo[r:r + 1] for r in range(_WIDE_R1)]
    counts += [taken_tail[r:r + 1] for r in range(PEER_TOPK - _WIDE_R1)]
    return counts, z, jnp.sum(taken_lo + taken_tail, axis=0, keepdims=True)


def _select_distinct(s1, s2):
    v1 = _top_values(s1, PEER_TOPK)
    v2 = _top_values(s2, PEER_TOPK)
    counts, z, taken = _top_pairs(v1, v2)
    most = jnp.maximum(jnp.maximum(_count_at_least(s1, v1[-1]), _count_at_least(s2, v2[-1])), taken)
    n1 = jnp.zeros(s1.shape, F32)
    rank2 = jnp.full(s2.shape, NOT_SELECTED_RANK, F32)
    for r in range(PEER_TOPK):
        n1 = jnp.where(s1 == v1[r], counts[r], n1)
        rank2 = jnp.where(s2 == v2[r], float(r), rank2)
    return (rank2, jnp.exp(s2 - v2[0]) / z, n1, jnp.exp(s1 - v1[0])), most


def _peer_sel_kernel(x_ref, wq_ref, keys_ref, r2_ref, b2_ref, n1_ref, a1_ref, s_ref):
    q = _dot(x_ref[...].astype(BF16), wq_ref[...]).astype(BF16)
    ts = q.shape[0]

    def store(h, cols, sel):
        rank2, b2, n1, a1 = sel
        r2_ref[h, :, cols] = rank2.astype(BF16)
        b2_ref[h, :, cols] = b2.astype(BF16)
        n1_ref[h, :, cols] = n1
        a1_ref[h, :, cols] = a1

    for h in range(PEER_HEADS):
        s_ref[0] = _dot_nt(keys_ref[2 * h], q[:, (2 * h) * PEER_HALF:(2 * h + 1) * PEER_HALF])
        s_ref[1] = _dot_nt(keys_ref[2 * h + 1], q[:, (2 * h + 1) * PEER_HALF:(2 * h + 2) * PEER_HALF])
        most = None
        for c in range(ts // LANES):
            cols = slice(c * LANES, (c + 1) * LANES)
            sel, m = _select_distinct(s_ref[0, :, cols], s_ref[1, :, cols])
            store(h, cols, sel)
            most = m if most is None else jnp.maximum(most, m)

        @pl.when(jnp.max(most) > float(PEER_TOPK))
        def _():
            store(h, slice(None), _select_exact(s_ref[0], s_ref[1]))


def _peer_sel(x1, wq, keys, ts):
    t = x1.shape[0]
    sel = pl.BlockSpec((PEER_HEADS, PEER_NKEYS, ts), lambda i: (0, 0, i))
    shp = lambda dt: jax.ShapeDtypeStruct((PEER_HEADS, PEER_NKEYS, t), dt)
    r2, b2, n1, a1 = pl.pallas_call(
        _peer_sel_kernel,
        grid=(t // ts,),
        in_specs=[pl.BlockSpec((ts, D_MODEL), lambda i: (i, 0)),
                  _resident((D_MODEL, PEER_HEADS * 2 * PEER_HALF)),
                  _resident((PEER_HEADS * 2, PEER_NKEYS, PEER_HALF))],
        out_specs=[sel] * 4,
        out_shape=[shp(BF16), shp(BF16), shp(F32), shp(F32)],
        scratch_shapes=[pltpu.VMEM((2, PEER_NKEYS, ts), F32)],
        compiler_params=_params("arbitrary"),
        name="peer_sel",
    )(x1, wq, keys)
    grouped = (PEER_HEADS, PEER_NKEYS // BF16_ROWS, BF16_ROWS, t)
    return r2.reshape(grouped), b2.reshape(grouped), n1, a1


def _expert_weights(r2_ref, b2_ref, n1_ref, a1_ref, ii, tm):
    w = None
    for h in range(PEER_HEADS):
        nrow = jnp.broadcast_to(n1_ref[h, ii:ii + 1, :], (BF16_ROWS, tm)).astype(BF16)
        arow = jnp.broadcast_to(a1_ref[h, ii:ii + 1, :], (BF16_ROWS, tm)).astype(BF16)
        wh = jnp.where(r2_ref[h] < nrow[None], b2_ref[h], jnp.zeros((), BF16)) * arow[None]
        w = wh if w is None else w + wh
    return w.reshape(PEER_NKEYS, tm)


def _peer_ffn_kernel(x_ref, r2_ref, b2_ref, n1_ref, a1_ref, u_ref, vt_ref, g_ref, b_ref, o_ref,
                     xt_ref, act_ref, pt_ref, acc_ref, *, rows_per_chunk):
    c = pl.program_id(1)
    tm = x_ref.shape[0]
    assert sum(PIECE_ROWS) == rows_per_chunk
    starts = [sum(PIECE_ROWS[:p]) for p in range(len(PIECE_ROWS) + 1)]

    @pl.when(c == 0)
    def _():
        xt_ref[...] = x_ref[...].T.astype(BF16)
        acc_ref[...] = jnp.zeros_like(acc_ref)

    def project(p):
        rows = slice(starts[p] * PEER_NKEYS, starts[p + 1] * PEER_NKEYS)
        act_ref[rows, :] = _dot(u_ref[rows, :], xt_ref[...])

    def weigh(p):
        for ii in range(starts[p], starts[p + 1]):
            rows = slice(ii * PEER_NKEYS, (ii + 1) * PEER_NKEYS)
            w = _expert_weights(r2_ref, b2_ref, n1_ref, a1_ref, ii, tm)
            pt_ref[rows, :] = w * _gelu(act_ref[rows, :]).astype(BF16)

    project(0)
    for p in range(1, len(PIECE_ROWS)):
        project(p)
        weigh(p - 1)
    weigh(len(PIECE_ROWS) - 1)
    acc_ref[...] += _dot(vt_ref[...], pt_ref[...])

    @pl.when(c == pl.num_programs(1) - 1)
    def _():
        o_ref[...] = _layernorm(DN_ALPHA * x_ref[...] + acc_ref[...].T, g_ref[...], b_ref[...])


def _peer_ffn(x1, r2, b2, n1, a1, u, vt, g, b, tm):
    t = x1.shape[0]
    ec = PEER_CHUNK
    rows_per_chunk = ec // PEER_NKEYS
    tok = pl.BlockSpec((tm, D_MODEL), lambda i, c: (i, 0))
    col = pl.BlockSpec((PEER_HEADS, PEER_NKEYS // BF16_ROWS, BF16_ROWS, tm), lambda i, c: (0, 0, 0, i))
    chunk_rows = pl.BlockSpec((PEER_HEADS, rows_per_chunk, tm), lambda i, c: (0, c, i))
    row = _resident((1, D_MODEL))
    return pl.pallas_call(
        functools.partial(_peer_ffn_kernel, rows_per_chunk=rows_per_chunk),
        grid=(t // tm, PEER_EXPERTS // ec),
        in_specs=[tok, col, col, chunk_rows, chunk_rows,
                  pl.BlockSpec((ec, D_MODEL), lambda i, c: (c, 0)),
                  pl.BlockSpec((D_MODEL, ec), lambda i, c: (0, c)), row, row],
        out_specs=tok,
        out_shape=jax.ShapeDtypeStruct((t, D_MODEL), F32),
        scratch_shapes=[pltpu.VMEM((D_MODEL, tm), BF16), pltpu.VMEM((ec, tm), F32), pltpu.VMEM((ec, tm), BF16),
                        pltpu.VMEM((D_MODEL, tm), F32)],
        compiler_params=_params("arbitrary", "arbitrary"),
        name="peer_ffn",
    )(x1, r2, b2, n1, a1, u, vt, g, b)


def _cast_t_kernel(v_ref, o_ref):
    o_ref[...] = v_ref[...].T.astype(BF16)


def _cast_transpose(v, tr=1024):
    rows, cols = v.shape
    return pl.pallas_call(
        _cast_t_kernel,
        grid=(rows // tr,),
        in_specs=[pl.BlockSpec((tr, cols), lambda i: (i, 0))],
        out_specs=pl.BlockSpec((cols, tr), lambda i: (0, i)),
        out_shape=jax.ShapeDtypeStruct((cols, rows), BF16),
        compiler_params=_params("arbitrary"),
        name="cast_transpose",
    )(v)


def _channel_sublayers(x, yc, yr, ya, wts, tm):
    x1 = _merge(x, yc, yr, ya, wts["in"], wts["br_conv"], wts["br_rnn"], wts["br_attn"], wts["o"],
                wts["ln1_g"], wts["ln1_b"], tm)
    r2, b2, n1, a1 = _peer_sel(x1, wts["peer_wq"], wts["peer_keys"], min(tm, 256))
    return _peer_ffn(x1, r2, b2, n1, a1, wts["peer_u"], wts["peer_vt"], wts["ln2_g"], wts["ln2_b"], tm)


def kernel(x_prompt, x_sample, mem_prompt, cache_mem_k, cache_mem_v, state_conv_z, state_rglru_conv, state_rglru_h, w_in, conv_w, rg_conv_w, rg_conv_b, rg_wa, rg_ba, rg_wx, rg_bx, rg_lambda, w_mk, w_mv, w_br_conv, w_br_rnn, w_br_attn, w_o, ln1_g, ln1_b, peer_wq, peer_keys, peer_u, peer_v, ln2_g, ln2_b):
    assert w_in.shape[0] == DEPTH == 1
    bp, sp, _ = x_prompt.shape
    bs = x_sample.shape[0]
    l = 0
    seq_width = MIX_WIDTH - ATTN_WIDTH
    win = w_in[l].astype(BF16)
    row = lambda a: a[l].reshape(1, -1)
    wg = jnp.concatenate([rg_wa[l], rg_wx[l]], axis=-1).astype(BF16)
    wts = {
        "in": win,
        "br_conv": w_br_conv[l].astype(BF16), "br_rnn": w_br_rnn[l].astype(BF16),
        "br_attn": w_br_attn[l].astype(BF16), "o": w_o[l].astype(BF16),
        "ln1_g": row(ln1_g), "ln1_b": row(ln1_b), "ln2_g": row(ln2_g), "ln2_b": row(ln2_b),
        "peer_wq": peer_wq[l].astype(BF16),
        "peer_keys": peer_keys[l].reshape(PEER_HEADS * 2, PEER_NKEYS, PEER_HALF).astype(BF16),
        "peer_u": peer_u[l].astype(BF16), "peer_vt": _cast_transpose(peer_v[l]),
    }
    rnn_args = (rg_conv_w[l], row(rg_conv_b), wg, row(rg_ba), row(rg_bx), row(rg_lambda))

    mk, mv = _mem_kv(mem_prompt.reshape(bp * N_MEM, D_MODEL), w_mk[l].astype(BF16), w_mv[l].astype(BF16))
    yc_p, cz_p = _conv_branch(x_prompt, win, conv_w[l])
    yr_p, rc_p, h_p = _rnn_branch(x_prompt, win, *rnn_args)
    ya_p = _attn_branch(x_prompt, win, mk.reshape(bp, N_MEM, ATTN_WIDTH), mv.reshape(bp, N_MEM, ATTN_WIDTH))
    tp = bp * sp
    y_p = _channel_sublayers(x_prompt.reshape(tp, D_MODEL), yc_p.reshape(tp, CONV_WIDTH),
                             yr_p.reshape(tp, RNN_WIDTH), ya_p.reshape(tp, ATTN_WIDTH), wts, 512)

    xs = x_sample.reshape(bs, D_MODEL)
    z_s = _proj(xs, win, MIX_WIDTH)
    yc_s, yr_s, cz_s, rc_s, h_s = _sample_seq(
        z_s, state_conv_z[l], state_rglru_conv[l], state_rglru_h[l], conv_w[l], *rnn_args)
    ya_s = _sample_attn(z_s[:, seq_width:].reshape(bs, ATTN_HEADS, ATTN_HEAD_DIM), cache_mem_k, cache_mem_v)
    ya_s = ya_s.reshape(bs, ATTN_WIDTH)
    y_s = _channel_sublayers(xs, yc_s, yr_s, ya_s, wts, bs)

    hd = (ATTN_HEADS, ATTN_HEAD_DIM)
    return (y_p.reshape(bp, sp, D_MODEL), y_s.reshape(bs, 1, D_MODEL),
            mk.reshape(1, bp, N_MEM, *hd), mv.reshape(1, bp, N_MEM, *hd),
            cz_p[None], rc_p[None], h_p.reshape(1, bp, RNN_WIDTH),
            cz_s[None], rc_s[None], h_s[None])
```

```python
import functools

import jax
import jax.numpy as jnp
from jax import lax
from jax.experimental import pallas as pl
from jax.experimental.pallas import tpu as pltpu

F32 = jnp.float32
BF16 = jnp.bfloat16

D_MODEL = 2048
N_MEM = 256
CONV_WIDTH = 1024
CONV_K = 3
RNN_WIDTH = 1024
RNN_HEADS = 8
RNN_HEAD_DIM = RNN_WIDTH // RNN_HEADS
RNN_CONV_K = 4
RG_C = 8.0
ATTN_HEADS = 4
ATTN_HEAD_DIM = 256
ATTN_WIDTH = ATTN_HEADS * ATTN_HEAD_DIM
N_BRANCH = 3
PEER_HEADS = 8
PEER_NKEYS = 128
PEER_EXPERTS = PEER_NKEYS * PEER_NKEYS
PEER_TOPK = 16
PEER_HALF = 128
PEER_CHUNK = 1024
PIECE_ROWS = (4, 4)
MIX_WIDTH = 3 * CONV_WIDTH + 2 * RNN_WIDTH + ATTN_WIDTH
RNN_X_BLOCK = 3 * CONV_WIDTH // RNN_WIDTH
RNN_GATE_BLOCK = RNN_X_BLOCK + 1
ATTN_Q_BLOCK = (3 * CONV_WIDTH + 2 * RNN_WIDTH) // ATTN_WIDTH
DEPTH = 1
DN_ALPHA = (2.0 * DEPTH) ** 0.25
LN_EPS = 1e-5

VMEM_LIMIT_V7X = 56 * 1024 * 1024
SUBLANES = 8
LANES = 128
BF16_ROWS = 2 * SUBLANES
NOT_SELECTED_RANK = 99.0


def _params(*sem):
    return pltpu.CompilerParams(dimension_semantics=sem, vmem_limit_bytes=VMEM_LIMIT_V7X)


def _resident(shape, index=None):
    index = (0,) * len(shape) if index is None else tuple(index)
    return pl.BlockSpec(shape, lambda *_: index, pipeline_mode=pl.Buffered(1))


def _dot(a, b):
    return jnp.dot(a, b, preferred_element_type=F32)


def _dot_nt(a, b):
    return lax.dot_general(a, b, (((1,), (1,)), ((), ())), preferred_element_type=F32)


def _sigmoid(x):
    return 1.0 / (1.0 + jnp.exp(-x))


def _gelu(x):
    return 0.5 * x * (1.0 + jnp.tanh(0.7978845608028654 * (x + 0.044715 * (x * x * x))))


def _softplus(x):
    return jnp.maximum(x, 0.0) + jnp.log1p(jnp.exp(-jnp.abs(x)))


def _layernorm(y, g, b):
    mu = jnp.mean(y, axis=-1, keepdims=True)
    yc = y - mu
    var = jnp.mean(yc * yc, axis=-1, keepdims=True)
    return yc * lax.rsqrt(var + LN_EPS) * g + b


def _mem_kv_kernel(m_ref, wk_ref, wv_ref, k_ref, v_ref):
    mb = m_ref[...].astype(BF16)
    k_ref[...] = _dot(mb, wk_ref[...])
    v_ref[...] = _dot(mb, wv_ref[...])


def _mem_kv(mem, wk, wv, tm=256):
    m = mem.shape[0]
    return pl.pallas_call(
        _mem_kv_kernel,
        grid=(m // tm,),
        in_specs=[pl.BlockSpec((tm, D_MODEL), lambda i: (i, 0)),
                  _resident((D_MODEL, ATTN_WIDTH)), _resident((D_MODEL, ATTN_WIDTH))],
        out_specs=[pl.BlockSpec((tm, ATTN_WIDTH), lambda i: (i, 0))] * 2,
        out_shape=[jax.ShapeDtypeStruct((m, ATTN_WIDTH), F32)] * 2,
        compiler_params=_params("arbitrary"),
        name="mem_kv",
    )(mem, wk, wv)


def _conv_kernel(x_ref, w_ref, cw_ref, y_ref, st_ref, pb_ref, *, ts):
    @pl.when(pl.program_id(1) == 0)
    def _():
        pb_ref[0:SUBLANES, :] = jnp.zeros((SUBLANES, CONV_WIDTH), F32)

    xb = x_ref[0].astype(BF16)
    p = _dot(xb, w_ref[:, CONV_WIDTH:2 * CONV_WIDTH]) * _dot(xb, w_ref[:, 2 * CONV_WIDTH:])
    pb_ref[SUBLANES:SUBLANES + ts, :] = p
    z_b = _dot(xb, w_ref[:, :CONV_WIDTH])
    p1 = pb_ref[SUBLANES - 1:SUBLANES - 1 + ts, :]
    p2 = pb_ref[SUBLANES - 2:SUBLANES - 2 + ts, :]
    cw = cw_ref[...]
    y = z_b * (cw[0:1] * p2 + cw[1:2] * p1 + cw[2:3] * p)
    y_ref[0] = y.astype(BF16)
    tail = pb_ref[ts:ts + SUBLANES, :]
    st_ref[0] = tail[SUBLANES - (CONV_K - 1):]
    pb_ref[0:SUBLANES, :] = tail


def _conv_branch(x, win, cw, ts=512):
    b, s, _ = x.shape
    return pl.pallas_call(
        functools.partial(_conv_kernel, ts=ts),
        grid=(b, s // ts),
        in_specs=[pl.BlockSpec((1, ts, D_MODEL), lambda i, j: (i, j, 0)),
                  _resident((D_MODEL, 3 * CONV_WIDTH)), _resident((CONV_K, CONV_WIDTH))],
        out_specs=[pl.BlockSpec((1, ts, CONV_WIDTH), lambda i, j: (i, j, 0)),
                   pl.BlockSpec((1, CONV_K - 1, CONV_WIDTH), lambda i, j: (i, 0, 0))],
        out_shape=[jax.ShapeDtypeStruct((b, s, CONV_WIDTH), BF16),
                   jax.ShapeDtypeStruct((b, CONV_K - 1, CONV_WIDTH), F32)],
        scratch_shapes=[pltpu.VMEM((SUBLANES + ts, CONV_WIDTH), F32)],
        compiler_params=_params("arbitrary", "arbitrary"),
        name="conv_branch",
    )(x, win, cw)


def _rglru_gates(xr, wg_ref, ba, bx, lam):
    xb = xr.astype(BF16)
    gr, gi = [], []
    for h in range(RNN_HEADS):
        g = _dot(xb[:, h * RNN_HEAD_DIM:(h + 1) * RNN_HEAD_DIM], wg_ref[h])
        gr.append(g[:, :RNN_HEAD_DIM])
        gi.append(g[:, RNN_HEAD_DIM:])
    r = _sigmoid(jnp.concatenate(gr, axis=1) + ba)
    i = _sigmoid(jnp.concatenate(gi, axis=1) + bx)
    log_a = -RG_C * r * _softplus(-lam)
    a = jnp.exp(log_a)
    th = jnp.tanh(log_a)
    w = -2.0 * th / (1.0 - th)
    u = jnp.where(w > 0.0, w * lax.rsqrt(w), 0.0) * (i * xr)
    return a, u


def _scan_rows(a, u, h_in, ts):
    pos = lax.broadcasted_iota(jnp.int32, a.shape, 0) % SUBLANES
    d = 1
    while d < SUBLANES:
        keep = pos >= d
        u = jnp.where(keep, u + a * pltpu.roll(u, d, axis=0), u)
        a = jnp.where(keep, a * pltpu.roll(a, d, axis=0), a)
        d *= 2
    groups = []
    h = h_in
    for g in range(ts // SUBLANES):
        rows = slice(g * SUBLANES, (g + 1) * SUBLANES)
        hg = a[rows] * h + u[rows]
        groups.append(hg)
        h = hg[SUBLANES - 1:SUBLANES]
    return jnp.concatenate(groups, axis=0)


def _rnn_kernel(x_ref, wx_ref, wgate_ref, cw_ref, cb_ref, wg_ref, ba_ref, bx_ref, lam_ref,
                y_ref, buf_ref, h_ref, rb_ref, hc_ref, *, ts):
    @pl.when(pl.program_id(1) == 0)
    def _():
        rb_ref[0:SUBLANES, :] = jnp.zeros((SUBLANES, RNN_WIDTH), F32)
        hc_ref[...] = jnp.zeros((1, RNN_WIDTH), F32)

    xb = x_ref[0].astype(BF16)
    rx = _dot(xb, wx_ref[...])
    rgate = _dot(xb, wgate_ref[...])
    rb_ref[SUBLANES:SUBLANES + ts, :] = rx
    cw = cw_ref[...]
    xr = cw[3:4] * rx + cb_ref[...]
    for k in range(RNN_CONV_K - 1):
        off = SUBLANES - (RNN_CONV_K - 1) + k
        xr = xr + cw[k:k + 1] * rb_ref[off:off + ts, :]
    a, u = _rglru_gates(xr, wg_ref, ba_ref[...], bx_ref[...], lam_ref[...])
    h = _scan_rows(a, u, hc_ref[...], ts)
    y_ref[0] = (h * _gelu(rgate)).astype(BF16)
    hc_ref[...] = h[ts - 1:ts, :]
    h_ref[0] = h[ts - 1:ts, :]
    tail = rb_ref[ts:ts + SUBLANES, :]
    buf_ref[0] = tail[SUBLANES - (RNN_CONV_K - 1):]
    rb_ref[0:SUBLANES, :] = tail


def _rnn_branch(x, win, cw, cb, wg, ba, bx, lam, ts=512):
    b, s, _ = x.shape
    row = _resident((1, RNN_WIDTH))
    return pl.pallas_call(
        functools.partial(_rnn_kernel, ts=ts),
        grid=(b, s // ts),
        in_specs=[pl.BlockSpec((1, ts, D_MODEL), lambda i, j: (i, j, 0)),
                  _resident((D_MODEL, RNN_WIDTH), (0, RNN_X_BLOCK)),
                  _resident((D_MODEL, RNN_WIDTH), (0, RNN_GATE_BLOCK)), _resident((RNN_CONV_K, RNN_WIDTH)), row,
                  _resident((RNN_HEADS, RNN_HEAD_DIM, 2 * RNN_HEAD_DIM)), row, row, row],
        out_specs=[pl.BlockSpec((1, ts, RNN_WIDTH), lambda i, j: (i, j, 0)),
                   pl.BlockSpec((1, RNN_CONV_K - 1, RNN_WIDTH), lambda i, j: (i, 0, 0)),
                   pl.BlockSpec((1, 1, RNN_WIDTH), lambda i, j: (i, 0, 0))],
        out_shape=[jax.ShapeDtypeStruct((b, s, RNN_WIDTH), BF16),
                   jax.ShapeDtypeStruct((b, RNN_CONV_K - 1, RNN_WIDTH), F32),
                   jax.ShapeDtypeStruct((b, 1, RNN_WIDTH), F32)],
        scratch_shapes=[pltpu.VMEM((SUBLANES + ts, RNN_WIDTH), F32), pltpu.VMEM((1, RNN_WIDTH), F32)],
        compiler_params=_params("arbitrary", "arbitrary"),
        name="rnn_branch",
    )(x, win, win, cw, cb, wg, ba, bx, lam)


def _attn_kernel(x_ref, w_ref, k_ref, v_ref, y_ref):
    q = _dot(x_ref[0].astype(BF16), w_ref[...]).astype(BF16)
    kb = k_ref[0].astype(BF16)
    vb = v_ref[0].astype(BF16)
    outs = []
    for h in range(ATTN_HEADS):
        sl = slice(h * ATTN_HEAD_DIM, (h + 1) * ATTN_HEAD_DIM)
        sc = _dot_nt(q[:, sl], kb[:, sl]) * (ATTN_HEAD_DIM ** -0.5)
        e = jnp.exp(sc - jnp.max(sc, axis=-1, keepdims=True))
        p = e / jnp.sum(e, axis=-1, keepdims=True)
        outs.append(_dot(p.astype(BF16), vb[:, sl]))
    y_ref[0] = jnp.concatenate(outs, axis=1).astype(BF16)


def _attn_branch(x, win, mk, mv, ts=512):
    b, s, _ = x.shape
    return pl.pallas_call(
        _attn_kernel,
        grid=(b, s // ts),
        in_specs=[pl.BlockSpec((1, ts, D_MODEL), lambda i, j: (i, j, 0)),
                  _resident((D_MODEL, ATTN_WIDTH), (0, ATTN_Q_BLOCK)),
                  pl.BlockSpec((1, N_MEM, ATTN_WIDTH), lambda i, j: (i, 0, 0)),
                  pl.BlockSpec((1, N_MEM, ATTN_WIDTH), lambda i, j: (i, 0, 0))],
        out_specs=pl.BlockSpec((1, ts, ATTN_WIDTH), lambda i, j: (i, j, 0)),
        out_shape=jax.ShapeDtypeStruct((b, s, ATTN_WIDTH), BF16),
        compiler_params=_params("arbitrary", "arbitrary"),
        name="attn_branch",
    )(x, win, mk, mv)


def _sample_seq_kernel(z_ref, cst_ref, rst_ref, h0_ref, ccw_ref, rcw_ref, rcb_ref, wg_ref,
                       ba_ref, bx_ref, lam_ref, yc_ref, yr_ref, cz_ref, rc_ref, h_ref):
    z = z_ref[...]
    p = z[:, CONV_WIDTH:2 * CONV_WIDTH] * z[:, 2 * CONV_WIDTH:3 * CONV_WIDTH]
    cst = cst_ref[...]
    ccw = ccw_ref[...]
    conv_y = ccw[0:1] * cst[:, 0, :] + ccw[1:2] * cst[:, 1, :] + ccw[2:3] * p
    yc_ref[...] = (z[:, :CONV_WIDTH] * conv_y).astype(BF16)
    cz_ref[:, 0, :] = cst[:, 1, :]
    cz_ref[:, 1, :] = p
    rx = z[:, 3 * CONV_WIDTH:3 * CONV_WIDTH + RNN_WIDTH]
    rst = rst_ref[...]
    rcw = rcw_ref[...]
    xr = rcw[0:1] * rst[:, 0, :] + rcw[1:2] * rst[:, 1, :] + rcw[2:3] * rst[:, 2, :] + rcw[3:4] * rx + rcb_ref[...]
    a, u = _rglru_gates(xr, wg_ref, ba_ref[...], bx_ref[...], lam_ref[...])
    h = a * h0_ref[...] + u
    yr_ref[...] = (h * _gelu(z[:, 3 * CONV_WIDTH + RNN_WIDTH:])).astype(BF16)
    h_ref[...] = h
    rc_ref[:, 0, :] = rst[:, 1, :]
    rc_ref[:, 1, :] = rst[:, 2, :]
    rc_ref[:, 2, :] = rx


def _sample_seq(z, cst, rst, h0, ccw, rcw, rcb, wg, ba, bx, lam, tb=32):
    n = z.shape[0]
    row = _resident((1, RNN_WIDTH))

    def tok(*tail):
        nd = len(tail)
        return pl.BlockSpec((tb,) + tail, lambda i: (i,) + (0,) * nd)

    return pl.pallas_call(
        _sample_seq_kernel,
        grid=(n // tb,),
        in_specs=[tok(3 * CONV_WIDTH + 2 * RNN_WIDTH), tok(CONV_K - 1, CONV_WIDTH),
                  tok(RNN_CONV_K - 1, RNN_WIDTH), tok(RNN_WIDTH),
                  _resident((CONV_K, CONV_WIDTH)), _resident((RNN_CONV_K, RNN_WIDTH)), row,
                  _resident((RNN_HEADS, RNN_HEAD_DIM, 2 * RNN_HEAD_DIM)), row, row, row],
        out_specs=[tok(CONV_WIDTH), tok(RNN_WIDTH),
                   tok(CONV_K - 1, CONV_WIDTH), tok(RNN_CONV_K - 1, RNN_WIDTH), tok(RNN_WIDTH)],
        out_shape=[jax.ShapeDtypeStruct((n, CONV_WIDTH), BF16), jax.ShapeDtypeStruct((n, RNN_WIDTH), BF16),
                   jax.ShapeDtypeStruct((n, CONV_K - 1, CONV_WIDTH), F32),
                   jax.ShapeDtypeStruct((n, RNN_CONV_K - 1, RNN_WIDTH), F32),
                   jax.ShapeDtypeStruct((n, RNN_WIDTH), F32)],
        compiler_params=_params("arbitrary"),
        name="sample_seq",
    )(z, cst, rst, h0, ccw, rcw, rcb, wg, ba, bx, lam)


def _sample_attn_kernel(q_ref, k_ref, v_ref, y_ref):
    q = q_ref[...]
    sc = jnp.sum(k_ref[0] * q[:, None], axis=-1, keepdims=True) * (ATTN_HEAD_DIM ** -0.5)
    e = jnp.exp(sc - jnp.max(sc, axis=1, keepdims=True))
    pr = e / jnp.sum(e, axis=1, keepdims=True)
    y_ref[...] = jnp.sum(pr * v_ref[0], axis=1).astype(BF16)


def _sample_attn(q, ck, cv, tb=4):
    n = q.shape[0]
    tok = pl.BlockSpec((tb, ATTN_HEADS, ATTN_HEAD_DIM), lambda i: (i, 0, 0))
    mem = pl.BlockSpec((1, tb, N_MEM, ATTN_HEADS, ATTN_HEAD_DIM), lambda i: (0, i, 0, 0, 0))
    return pl.pallas_call(
        _sample_attn_kernel,
        grid=(n // tb,),
        in_specs=[tok, mem, mem],
        out_specs=tok,
        out_shape=jax.ShapeDtypeStruct((n, ATTN_HEADS, ATTN_HEAD_DIM), BF16),
        compiler_params=_params("arbitrary"),
        name="sample_attn",
    )(q, ck, cv)


def _proj_kernel(x_ref, w_ref, o_ref):
    o_ref[...] = _dot(x_ref[...].astype(BF16), w_ref[...])


def _proj(x, w, n, tn=1024):
    m, k = x.shape
    return pl.pallas_call(
        _proj_kernel,
        grid=(n // tn,),
        in_specs=[_resident((m, k)), pl.BlockSpec((k, tn), lambda j: (0, j))],
        out_specs=pl.BlockSpec((m, tn), lambda j: (0, j)),
        out_shape=jax.ShapeDtypeStruct((m, n), F32),
        compiler_params=_params("arbitrary"),
        name="sample_proj",
    )(x, w)


def _gate_kernel(x_ref, yc_ref, yr_ref, ya_ref, wg0_ref, wg1_ref, wg2_ref, wc_ref, wr_ref, wa_ref, m_ref):
    xb = x_ref[...].astype(BF16)
    merged = (_sigmoid(_dot(xb, wg0_ref[...])) * _dot(yc_ref[...], wc_ref[...])
              + _sigmoid(_dot(xb, wg1_ref[...])) * _dot(yr_ref[...], wr_ref[...])
              + _sigmoid(_dot(xb, wg2_ref[...])) * _dot(ya_ref[...], wa_ref[...]))
    m_ref[...] = merged.astype(BF16)


def _out_proj_kernel(x_ref, m_ref, wo_ref, g_ref, b_ref, o_ref):
    o_ref[...] = _layernorm(DN_ALPHA * x_ref[...] + _dot(m_ref[...], wo_ref[...]), g_ref[...], b_ref[...])


def _merge(x, yc, yr, ya, win, wc, wr, wa, wo, g, b, tm, tn=1024):
    t = x.shape[0]
    nb = D_MODEL // tn
    gate0 = (win.shape[1] - N_BRANCH * D_MODEL) // tn
    once = pl.Buffered(1)
    gate = [pl.BlockSpec((D_MODEL, tn), functools.partial(lambda n, i, k: (0, gate0 + k * nb + n), k=k),
                         pipeline_mode=once) for k in range(N_BRANCH)]
    br = pl.BlockSpec((CONV_WIDTH, tn), lambda n, i: (0, n), pipeline_mode=once)
    tok_h = pl.BlockSpec((tm, CONV_WIDTH), lambda n, i: (i, 0))
    merged = pl.pallas_call(
        _gate_kernel,
        grid=(nb, t // tm),
        in_specs=[pl.BlockSpec((tm, D_MODEL), lambda n, i: (i, 0)), tok_h, tok_h, tok_h] + gate + [br, br, br],
        out_specs=pl.BlockSpec((tm, tn), lambda n, i: (i, n)),
        out_shape=jax.ShapeDtypeStruct((t, D_MODEL), BF16),
        compiler_params=_params("arbitrary", "arbitrary"),
        name="gated_merge",
    )(x, yc, yr, ya, win, win, win, wc, wr, wa)
    tok = pl.BlockSpec((tm, D_MODEL), lambda i: (i, 0))
    row = _resident((1, D_MODEL))
    return pl.pallas_call(
        _out_proj_kernel,
        grid=(t // tm,),
        in_specs=[tok, tok, _resident((D_MODEL, D_MODEL)), row, row],
        out_specs=tok,
        out_shape=jax.ShapeDtypeStruct((t, D_MODEL), F32),
        compiler_params=_params("arbitrary"),
        name="out_proj",
    )(x, merged, wo, g, b)


def _top_rows(sc, k):
    n = sc.shape[0]
    rows = lax.broadcasted_iota(jnp.int32, sc.shape, 0)
    rank = jnp.full(sc.shape, NOT_SELECTED_RANK, F32)
    rem = sc
    vals = []
    for r in range(k):
        m = jnp.max(rem, axis=0, keepdims=True)
        first = jnp.min(jnp.where(rem == m, rows, n), axis=0, keepdims=True)
        hit = rows == first
        rank = jnp.where(hit, float(r), rank)
        rem = jnp.where(hit, -jnp.inf, rem)
        vals.append(m)
    return jnp.concatenate(vals, axis=0), rank


_WIDE_R1 = 8


def _candidates(v1, v2):
    blocks = [v1[0:1] + v2]
    blocks += [v1[r:r + 1] + v2[:SUBLANES] for r in range(1, _WIDE_R1)]
    blocks.append(v1[_WIDE_R1:] + v2[0:1])
    return jnp.concatenate(blocks, axis=0)


def _row_counts(chosen):
    cnt = jnp.where(chosen, 1.0, 0.0)
    counts = [jnp.sum(cnt[0:PEER_TOPK], axis=0, keepdims=True)]
    for r in range(1, _WIDE_R1):
        lo = PEER_TOPK + (r - 1) * SUBLANES
        counts.append(jnp.sum(cnt[lo:lo + SUBLANES], axis=0, keepdims=True))
    lo = PEER_TOPK + (_WIDE_R1 - 1) * SUBLANES
    counts += [cnt[lo + r:lo + r + 1] for r in range(PEER_TOPK - _WIDE_R1)]
    return counts


def _select_exact(s1, s2):
    v1, rank1 = _top_rows(s1, PEER_TOPK)
    v2, rank2 = _top_rows(s2, PEER_TOPK)
    cand = _candidates(v1, v2)
    _, crank = _top_rows(cand, PEER_TOPK)
    chosen = crank < NOT_SELECTED_RANK
    z = jnp.sum(jnp.where(chosen, jnp.exp(cand - cand[0:1]), 0.0), axis=0, keepdims=True)
    counts = _row_counts(chosen)
    n1 = jnp.zeros(s1.shape, F32)
    for r in range(PEER_TOPK):
        n1 = jnp.where(rank1 == float(r), counts[r], n1)
    return rank2, jnp.exp(s2 - v2[0:1]) / z, n1, jnp.exp(s1 - v1[0:1])


def _sorting_network(lo, hi):
    def merge(lo, hi, r):
        step = r * 2
        if step < hi - lo:
            yield from merge(lo, hi, step)
            yield from merge(lo + r, hi, step)
            yield from ((i, i + r) for i in range(lo + r, hi - r, step))
        else:
            yield (lo, lo + r)

    if hi - lo >= 1:
        mid = lo + (hi - lo) // 2
        yield from _sorting_network(lo, mid)
        yield from _sorting_network(mid + 1, hi)
        yield from merge(lo, hi, 1)


def _top_values(sc, k):
    assert sc.shape[0] == SUBLANES * k
    x = [sc[SUBLANES * v:SUBLANES * (v + 1)] for v in range(k)]
    for a, b in _sorting_network(0, k - 1):
        x[a], x[b] = jnp.maximum(x[a], x[b]), jnp.minimum(x[a], x[b])
    vals = []
    for r in range(k):
        m = jnp.max(x[0], axis=0, keepdims=True)
        vals.append(m)
        live = k - r
        if live > 1:
            hit = x[0] == m
            for p in range(live - 1):
                x[p] = jnp.where(hit, x[p + 1], x[p])
    return vals


def _count_at_least(sc, v):
    return jnp.sum(jnp.where(sc >= v, 1.0, 0.0), axis=0, keepdims=True)


def _top_pairs(v1, v2):
    lo = jnp.concatenate(v1[:_WIDE_R1], axis=0)
    x = [lo + v2[p] for p in range(PEER_TOPK)]
    tail = jnp.concatenate(v1[_WIDE_R1:], axis=0) + v2[0]
    top = v1[0] + v2[0]
    taken_lo = jnp.zeros_like(lo)
    taken_tail = jnp.zeros_like(lo)
    z = jnp.zeros_like(top)
    for r in range(PEER_TOPK):
        m = jnp.max(jnp.maximum(x[0], tail), axis=0, keepdims=True)
        hit, hit_tail = x[0] == m, tail == m
        taken_lo = taken_lo + jnp.where(hit, 1.0, 0.0)
        taken_tail = taken_tail + jnp.where(hit_tail, 1.0, 0.0)
        z = z + jnp.exp(m - top)
        live = PEER_TOPK - r
        if live > 1:
            for p in range(live - 1):
                x[p] = jnp.where(hit, x[p + 1], x[p])
            tail = jnp.where(hit_tail, -jnp.inf, tail)
    counts = [taken_lo[r:r + 1] for r in range(_WIDE_R1)]
    counts += [taken_tail[r:r + 1] for r in range(PEER_TOPK - _WIDE_R1)]
    return counts, z, jnp.sum(taken_lo + taken_tail, axis=0, keepdims=True)


def _select_distinct(s1, s2):
    v1 = _top_values(s1, PEER_TOPK)
    v2 = _top_values(s2, PEER_TOPK)
    counts, z, taken = _top_pairs(v1, v2)
    most = jnp.maximum(jnp.maximum(_count_at_least(s1, v1[-1]), _count_at_least(s2, v2[-1])), taken)
    n1 = jnp.zeros(s1.shape, F32)
    rank2 = jnp.full(s2.shape, NOT_SELECTED_RANK, F32)
    for r in range(PEER_TOPK):
        n1 = jnp.where(s1 == v1[r], counts[r], n1)
        rank2 = jnp.where(s2 == v2[r], float(r), rank2)
    return (rank2, jnp.exp(s2 - v2[0]) / z, n1, jnp.exp(s1 - v1[0])), most


def _peer_sel_kernel(x_ref, wq_ref, keys_ref, r2_ref, b2_ref, n1_ref, a1_ref, s_ref):
    q = _dot(x_ref[...].astype(BF16), wq_ref[...]).astype(BF16)
    ts = q.shape[0]

    def store(h, cols, sel):
        rank2, b2, n1, a1 = sel
        r2_ref[h, :, cols] = rank2.astype(BF16)
        b2_ref[h, :, cols] = b2.astype(BF16)
        n1_ref[h, :, cols] = n1
        a1_ref[h, :, cols] = a1

    for h in range(PEER_HEADS):
        s_ref[0] = _dot_nt(keys_ref[2 * h], q[:, (2 * h) * PEER_HALF:(2 * h + 1) * PEER_HALF])
        s_ref[1] = _dot_nt(keys_ref[2 * h + 1], q[:, (2 * h + 1) * PEER_HALF:(2 * h + 2) * PEER_HALF])
        most = None
        for c in range(ts // LANES):
            cols = slice(c * LANES, (c + 1) * LANES)
            sel, m = _select_distinct(s_ref[0, :, cols], s_ref[1, :, cols])
            store(h, cols, sel)
            most = m if most is None else jnp.maximum(most, m)

        @pl.when(jnp.max(most) > float(PEER_TOPK))
        def _():
            store(h, slice(None), _select_exact(s_ref[0], s_ref[1]))


def _peer_sel(x1, wq, keys, ts):
    t = x1.shape[0]
    sel = pl.BlockSpec((PEER_HEADS, PEER_NKEYS, ts), lambda i: (0, 0, i))
    shp = lambda dt: jax.ShapeDtypeStruct((PEER_HEADS, PEER_NKEYS, t), dt)
    r2, b2, n1, a1 = pl.pallas_call(
        _peer_sel_kernel,
        grid=(t // ts,),
        in_specs=[pl.BlockSpec((ts, D_MODEL), lambda i: (i, 0)),
                  _resident((D_MODEL, PEER_HEADS * 2 * PEER_HALF)),
                  _resident((PEER_HEADS * 2, PEER_NKEYS, PEER_HALF))],
        out_specs=[sel] * 4,
        out_shape=[shp(BF16), shp(BF16), shp(F32), shp(F32)],
        scratch_shapes=[pltpu.VMEM((2, PEER_NKEYS, ts), F32)],
        compiler_params=_params("arbitrary"),
        name="peer_sel",
    )(x1, wq, keys)
    grouped = (PEER_HEADS, PEER_NKEYS // BF16_ROWS, BF16_ROWS, t)
    return r2.reshape(grouped), b2.reshape(grouped), n1, a1


def _expert_weights(r2_ref, b2_ref, n1_ref, a1_ref, ii, cols):
    width = cols.stop - cols.start
    w = None
    for h in range(PEER_HEADS):
        nrow = jnp.broadcast_to(n1_ref[h, ii:ii + 1, cols], (BF16_ROWS, width)).astype(BF16)
        arow = jnp.broadcast_to(a1_ref[h, ii:ii + 1, cols], (BF16_ROWS, width)).astype(BF16)
        wh = jnp.where(r2_ref[h, :, :, cols] < nrow[None], b2_ref[h, :, :, cols], jnp.zeros((), BF16)) * arow[None]
        w = wh if w is None else w + wh
    return w.reshape(PEER_NKEYS, width)


def _peer_ffn_kernel(x_ref, r2_ref, b2_ref, n1_ref, a1_ref, u_ref, vt_ref, g_ref, b_ref, o_ref,
                     xt_ref, act_ref, pt_ref, acc_ref, *, rows_per_chunk):
    c = pl.program_id(1)
    tm = x_ref.shape[0]
    assert sum(PIECE_ROWS) == rows_per_chunk
    starts = [sum(PIECE_ROWS[:p]) for p in range(len(PIECE_ROWS) + 1)]

    @pl.when(c == 0)
    def _():
        xt_ref[...] = x_ref[...].T.astype(BF16)
        acc_ref[...] = jnp.zeros_like(acc_ref)

    def project(p):
        rows = slice(starts[p] * PEER_NKEYS, starts[p + 1] * PEER_NKEYS)
        act_ref[rows, :] = _dot(u_ref[rows, :], xt_ref[...])

    def weigh(p):
        for ii in range(starts[p], starts[p + 1]):
            rows = slice(ii * PEER_NKEYS, (ii + 1) * PEER_NKEYS)
            half = max(tm // 2, LANES)
            for c0 in range(0, tm, half):
                cols = slice(c0, c0 + half)
                w = _expert_weights(r2_ref, b2_ref, n1_ref, a1_ref, ii, cols)
                pt_ref[rows, cols] = w * _gelu(act_ref[rows, cols]).astype(BF16)

    project(0)
    for p in range(1, len(PIECE_ROWS)):
        project(p)
        weigh(p - 1)
    weigh(len(PIECE_ROWS) - 1)
    acc_ref[...] += _dot(vt_ref[...], pt_ref[...])

    @pl.when(c == pl.num_programs(1) - 1)
    def _():
        o_ref[...] = _layernorm(DN_ALPHA * x_ref[...] + acc_ref[...].T, g_ref[...], b_ref[...])


def _peer_ffn(x1, r2, b2, n1, a1, u, vt, g, b, tm):
    t = x1.shape[0]
    ec = PEER_CHUNK
    rows_per_chunk = ec // PEER_NKEYS
    tok = pl.BlockSpec((tm, D_MODEL), lambda i, c: (i, 0))
    col = pl.BlockSpec((PEER_HEADS, PEER_NKEYS // BF16_ROWS, BF16_ROWS, tm), lambda i, c: (0, 0, 0, i))
    chunk_rows = pl.BlockSpec((PEER_HEADS, rows_per_chunk, tm), lambda i, c: (0, c, i))
    row = _resident((1, D_MODEL))
    return pl.pallas_call(
        functools.partial(_peer_ffn_kernel, rows_per_chunk=rows_per_chunk),
        grid=(t // tm, PEER_EXPERTS // ec),
        in_specs=[tok, col, col, chunk_rows, chunk_rows,
                  pl.BlockSpec((ec, D_MODEL), lambda i, c: (c, 0)),
                  pl.BlockSpec((D_MODEL, ec), lambda i, c: (0, c)), row, row],
        out_specs=tok,
        out_shape=jax.ShapeDtypeStruct((t, D_MODEL), F32),
        scratch_shapes=[pltpu.VMEM((D_MODEL, tm), BF16), pltpu.VMEM((ec, tm), F32), pltpu.VMEM((ec, tm), BF16),
                        pltpu.VMEM((D_MODEL, tm), F32)],
        compiler_params=_params("arbitrary", "arbitrary"),
        name="peer_ffn",
    )(x1, r2, b2, n1, a1, u, vt, g, b)


def _cast_t_kernel(v_ref, o_ref):
    o_ref[...] = v_ref[...].T.astype(BF16)


def _cast_transpose(v, tr=1024):
    rows, cols = v.shape
    return pl.pallas_call(
        _cast_t_kernel,
        grid=(rows // tr,),
        in_specs=[pl.BlockSpec((tr, cols), lambda i: (i, 0))],
        out_specs=pl.BlockSpec((cols, tr), lambda i: (0, i)),
        out_shape=jax.ShapeDtypeStruct((cols, rows), BF16),
        compiler_params=_params("arbitrary"),
        name="cast_transpose",
    )(v)


def _channel_sublayers(x, yc, yr, ya, wts, tm):
    x1 = _merge(x, yc, yr, ya, wts["in"], wts["br_conv"], wts["br_rnn"], wts["br_attn"], wts["o"],
                wts["ln1_g"], wts["ln1_b"], tm)
    r2, b2, n1, a1 = _peer_sel(x1, wts["peer_wq"], wts["peer_keys"], min(tm, 256))
    return _peer_ffn(x1, r2, b2, n1, a1, wts["peer_u"], wts["peer_vt"], wts["ln2_g"], wts["ln2_b"], tm)


def kernel(x_prompt, x_sample, mem_prompt, cache_mem_k, cache_mem_v, state_conv_z, state_rglru_conv, state_rglru_h, w_in, conv_w, rg_conv_w, rg_conv_b, rg_wa, rg_ba, rg_wx, rg_bx, rg_lambda, w_mk, w_mv, w_br_conv, w_br_rnn, w_br_attn, w_o, ln1_g, ln1_b, peer_wq, peer_keys, peer_u, peer_v, ln2_g, ln2_b):
    assert w_in.shape[0] == DEPTH == 1
    bp, sp, _ = x_prompt.shape
    bs = x_sample.shape[0]
    l = 0
    seq_width = MIX_WIDTH - ATTN_WIDTH
    win = w_in[l].astype(BF16)
    row = lambda a: a[l].reshape(1, -1)
    wg = jnp.concatenate([rg_wa[l], rg_wx[l]], axis=-1).astype(BF16)
    wts = {
        "in": win,
        "br_conv": w_br_conv[l].astype(BF16), "br_rnn": w_br_rnn[l].astype(BF16),
        "br_attn": w_br_attn[l].astype(BF16), "o": w_o[l].astype(BF16),
        "ln1_g": row(ln1_g), "ln1_b": row(ln1_b), "ln2_g": row(ln2_g), "ln2_b": row(ln2_b),
        "peer_wq": peer_wq[l].astype(BF16),
        "peer_keys": peer_keys[l].reshape(PEER_HEADS * 2, PEER_NKEYS, PEER_HALF).astype(BF16),
        "peer_u": peer_u[l].astype(BF16), "peer_vt": _cast_transpose(peer_v[l]),
    }
    rnn_args = (rg_conv_w[l], row(rg_conv_b), wg, row(rg_ba), row(rg_bx), row(rg_lambda))

    mk, mv = _mem_kv(mem_prompt.reshape(bp * N_MEM, D_MODEL), w_mk[l].astype(BF16), w_mv[l].astype(BF16))
    yc_p, cz_p = _conv_branch(x_prompt, win, conv_w[l])
    yr_p, rc_p, h_p = _rnn_branch(x_prompt, win, *rnn_args)
    ya_p = _attn_branch(x_prompt, win, mk.reshape(bp, N_MEM, ATTN_WIDTH), mv.reshape(bp, N_MEM, ATTN_WIDTH))
    tp = bp * sp
    y_p = _channel_sublayers(x_prompt.reshape(tp, D_MODEL), yc_p.reshape(tp, CONV_WIDTH),
                             yr_p.reshape(tp, RNN_WIDTH), ya_p.reshape(tp, ATTN_WIDTH), wts, 512)

    xs = x_sample.reshape(bs, D_MODEL)
    z_s = _proj(xs, win, MIX_WIDTH)
    yc_s, yr_s, cz_s, rc_s, h_s = _sample_seq(
        z_s, state_conv_z[l], state_rglru_conv[l], state_rglru_h[l], conv_w[l], *rnn_args)
    ya_s = _sample_attn(z_s[:, seq_width:].reshape(bs, ATTN_HEADS, ATTN_HEAD_DIM), cache_mem_k, cache_mem_v)
    ya_s = ya_s.reshape(bs, ATTN_WIDTH)
    y_s = _channel_sublayers(xs, yc_s, yr_s, ya_s, wts, bs)

    hd = (ATTN_HEADS, ATTN_HEAD_DIM)
    return (y_p.reshape(bp, sp, D_MODEL), y_s.reshape(bs, 1, D_MODEL),
            mk.reshape(1, bp, N_MEM, *hd), mv.reshape(1, bp, N_MEM, *hd),
            cz_p[None], rc_p[None], h_p.reshape(1, bp, RNN_WIDTH),
            cz_s[None], rc_s[None], h_s[None])
```
